```python
import jax, jax.numpy as jnp
from jax import lax
import numpy as np

D_MODEL = 2048
BATCH = 32
SEQ = 256
DEPTH = 4
DEC_BATCH = 2
DEC_SEQ = 4096
PAST_LEN = 512

GRID_W = 64
D_MIX = D_MODEL
GROUP_W = D_MIX // 4
SSD_D_INNER = GROUP_W
SSD_HEADDIM = 64
SSD_HEADS = SSD_D_INNER // SSD_HEADDIM
SSD_GROUPS = 2
SSD_D_STATE = 128
SSD_CONV = 3
SSD_CHUNK = 128
SSD_XBC = SSD_D_INNER + 2 * SSD_GROUPS * SSD_D_STATE
HEAD_DIM = 64
ATT_HEADS = GROUP_W // HEAD_DIM
ATT_KV_HEADS = 2
Q_BLOCK = 128
ROPE_THETA = 10000.0
ROPE_AXIS_DIM = HEAD_DIM // 2
CONV_CH = GROUP_W
CONV_WIDTH = 31
FNET_CH = GROUP_W
FNET_GROUPS = 4
FNET_GROUP_CH = FNET_CH // FNET_GROUPS
IN_SIZES = (SSD_D_INNER, SSD_XBC, 2 * SSD_HEADS, ATT_HEADS * HEAD_DIM, ATT_KV_HEADS * HEAD_DIM,
            ATT_KV_HEADS * HEAD_DIM, 2 * CONV_CH, FNET_CH)
D_IN = SSD_D_INNER + SSD_XBC + 2 * SSD_HEADS + (ATT_HEADS + 2 * ATT_KV_HEADS) * HEAD_DIM + 2 * CONV_CH + FNET_CH
N_EXPERT_GROUPS = 4
EXPERTS_PER_GROUP = 4
N_EXPERTS = N_EXPERT_GROUPS * EXPERTS_PER_GROUP
EXPERT_FF = 256
TOP_K_IN_GROUP = 2
N_MOD = 6
EPS = 1e-6

kernel_name = 'hybrid_dit_prefix_ctx_step'


def rmsnorm(x, g):
    xf = x.astype(jnp.float32)
    y = xf * lax.rsqrt(jnp.mean(xf * xf, axis=-1, keepdims=True) + EPS)
    return (y * g.astype(jnp.float32)).astype(x.dtype)


def layernorm(x, g, b):
    xf = x.astype(jnp.float32)
    mu = jnp.mean(xf, axis=-1, keepdims=True)
    xc = xf - mu
    y = xc * lax.rsqrt(jnp.mean(xc * xc, axis=-1, keepdims=True) + EPS)
    return (y * g.astype(jnp.float32) + b.astype(jnp.float32)).astype(x.dtype)


def dwconv(x, w, b):
    pad = w.shape[0] // 2
    y = lax.conv_general_dilated(x, w[:, None, :].astype(x.dtype), window_strides=(1,),
                                 padding=[(pad, pad)], dimension_numbers=('NWC', 'WIO', 'NWC'),
                                 feature_group_count=x.shape[-1])
    return y + b.astype(x.dtype)


def split_in(u):
    return jnp.split(u, np.cumsum(IN_SIZES)[:-1].tolist(), axis=-1)


def adaln(cvec, w, b):
    m = jax.nn.silu(cvec) @ w + b
    return m.reshape(cvec.shape[0], N_MOD, D_MODEL)


def modulate(h, mod, i):
    return h * (1 + mod[:, 3 * i + 1][:, None, :]) + mod[:, 3 * i][:, None, :]


def rope_axis(x, pos):
    half = ROPE_AXIS_DIM // 2
    freqs = ROPE_THETA ** (-jnp.arange(half, dtype=jnp.float32) / half)
    ang = pos.astype(jnp.float32)[:, None] * freqs[None, :]
    cos = jnp.cos(ang)[None, :, None, :].astype(x.dtype)
    sin = jnp.sin(ang)[None, :, None, :].astype(x.dtype)
    x1, x2 = x[..., :half], x[..., half:]
    return jnp.concatenate([x1 * cos - x2 * sin, x1 * sin + x2 * cos], axis=-1)


def rope_2d(x, row, col):
    return jnp.concatenate([rope_axis(x[..., :ROPE_AXIS_DIM], row), rope_axis(x[..., ROPE_AXIS_DIM:], col)], axis=-1)


def ssd_scan(x, dt, a_coef, bm, cm, h0):
    f32 = jnp.float32
    b, L, H, P = x.shape
    nc = L // SSD_CHUNK
    rep = H // bm.shape[2]
    bm = jnp.repeat(bm.astype(f32), rep, axis=2).reshape(b, nc, SSD_CHUNK, H, -1)
    cm = jnp.repeat(cm.astype(f32), rep, axis=2).reshape(b, nc, SSD_CHUNK, H, -1)
    dt = dt.reshape(b, nc, SSD_CHUNK, H)
    xdt = x.astype(f32).reshape(b, nc, SSD_CHUNK, H, P) * dt[..., None]
    a_cs = jnp.cumsum(dt * a_coef, axis=2)
    seg = a_cs[:, :, :, None, :] - a_cs[:, :, None, :, :]
    lower = jnp.tril(jnp.ones((SSD_CHUNK, SSD_CHUNK), dtype=bool))[None, None, :, :, None]
    decay = jnp.exp(jnp.where(lower, seg, -jnp.inf))
    scores = jnp.einsum('bclhn,bcshn->bclsh', cm, bm) * decay
    y_diag = jnp.einsum('bclsh,bcshp->bclhp', scores, xdt)
    to_end = jnp.exp(a_cs[:, :, -1:, :] - a_cs)
    chunk_states = jnp.einsum('bclhn,bclh,bclhp->bchpn', bm, to_end, xdt)
    chunk_decay = jnp.exp(a_cs[:, :, -1, :])

    def step(h, inp):
        dec, st = inp
        return dec[:, :, None, None] * h + st, h

    h_final, h_start = lax.scan(step, h0.astype(f32),
                                (jnp.moveaxis(chunk_decay, 1, 0), jnp.moveaxis(chunk_states, 1, 0)))
    h_start = jnp.moveaxis(h_start, 0, 1)
    y_off = jnp.einsum('bclhn,bchpn,bclh->bclhp', cm, h_start, jnp.exp(a_cs))
    return (y_diag + y_off).reshape(b, L, H, P), h_final


def ssd_bidir(z, xbc, dt_raw, h0, p):
    b, L, _ = xbc.shape
    xbc = jax.nn.silu(dwconv(xbc, p['ssd_conv_w'], p['ssd_conv_b']))
    xs, bm, cm = jnp.split(xbc, [SSD_D_INNER, SSD_D_INNER + SSD_GROUPS * SSD_D_STATE], axis=-1)
    xs = xs.reshape(b, L, SSD_HEADS, SSD_HEADDIM)
    bm = bm.reshape(b, L, SSD_GROUPS, SSD_D_STATE)
    cm = cm.reshape(b, L, SSD_GROUPS, SSD_D_STATE)
    dt = jax.nn.softplus(dt_raw.astype(jnp.float32).reshape(b, L, 2, SSD_HEADS)
                         + p['ssd_dt_bias'].astype(jnp.float32))
    a_coef = -jnp.exp(p['ssd_A_log'].astype(jnp.float32))
    y_f, h_f = ssd_scan(xs, dt[:, :, 0], a_coef[0], bm, cm, h0[:, 0])
    y_b, h_b = ssd_scan(xs[:, ::-1], dt[:, ::-1, 1], a_coef[1], bm[:, ::-1], cm[:, ::-1], h0[:, 1])
    y = y_f + y_b[:, ::-1] + xs.astype(jnp.float32) * p['ssd_D'].astype(jnp.float32)[:, None]
    y = y.reshape(b, L, SSD_D_INNER).astype(z.dtype)
    y = rmsnorm(y * jax.nn.silu(z), p['ssd_norm'])
    return y, jnp.stack([h_f, h_b], axis=1)


def blocked_attention(q, k, v):
    b, L, H, D = q.shape
    kvh = k.shape[2]
    g = H // kvh
    nb = L // Q_BLOCK
    qb = jnp.moveaxis(q.reshape(b, nb, Q_BLOCK, kvh, g, D), 1, 0)
    scale = D ** -0.5

    def one_block(qi):
        s = jnp.einsum('bqkgd,bskd->bkgqs', qi, k).astype(jnp.float32) * scale
        pr = jax.nn.softmax(s, axis=-1).astype(v.dtype)
        return jnp.einsum('bkgqs,bskd->bqkgd', pr, v)

    o = lax.map(one_block, qb)
    return jnp.moveaxis(o, 0, 1).reshape(b, L, H * D)


def conformer_conv(glu, p):
    a, g = jnp.split(glu, 2, axis=-1)
    u = a * jax.nn.sigmoid(g)
    u = dwconv(u, p['cf_dw_w'], p['cf_dw_b'])
    u = layernorm(u, p['cf_ln_g'], p['cf_ln_b'])
    return jax.nn.silu(u)


def fourier_mix(u):
    b, L, _ = u.shape
    uf = u.astype(jnp.float32).reshape(b, L, FNET_GROUPS, FNET_GROUP_CH)
    y = jnp.fft.fft2(uf, axes=(1, 3), norm='ortho').real
    return y.astype(u.dtype).reshape(b, L, FNET_CH)


def mixer(h, p, h0, pos=None, ctx_kv=None):
    b, L, _ = h.shape
    z, xbc, dt_raw, q, k, v, glu, four = split_in(h @ p['w_in'])
    y_ssd, states = ssd_bidir(z, xbc, dt_raw, h0, p)
    q = rmsnorm(q.reshape(b, L, ATT_HEADS, HEAD_DIM), p['q_norm'])
    k = rmsnorm(k.reshape(b, L, ATT_KV_HEADS, HEAD_DIM), p['k_norm'])
    v = v.reshape(b, L, ATT_KV_HEADS, HEAD_DIM)
    if pos is None:
        k_all, v_all = k, v
    else:
        row, col = pos
        q = rope_2d(q, row, col)
        k = rope_2d(k, row, col)
        k_all = jnp.concatenate([ctx_kv[0].astype(k.dtype), k], axis=1)
        v_all = jnp.concatenate([ctx_kv[1].astype(v.dtype), v], axis=1)
    y_att = blocked_attention(q, k_all, v_all)
    y_conv = conformer_conv(glu, p)
    y_four = fourier_mix(four)
    y = jnp.concatenate([y_ssd, y_att, y_conv, y_four], axis=-1) @ p['w_out']
    return y, (k, v, states)


def hier_moe(h, p):
    b, L, D = h.shape
    t = h.reshape(b * L, D)
    gl = (t @ p['rg_w'] + p['rg_b']).astype(jnp.float32)
    gp = jax.nn.softmax(gl, axis=-1)
    _, gi = lax.top_k(gl, 1)
    gw = jnp.take_along_axis(gp, gi, axis=1)
    el = (t @ p['re_w'] + p['re_b']).astype(jnp.float32).reshape(-1, N_EXPERT_GROUPS, EXPERTS_PER_GROUP)
    el = jnp.take_along_axis(el, gi[:, :, None], axis=1)[:, 0]
    ev, ei = lax.top_k(el, TOP_K_IN_GROUP)
    ew = jax.nn.softmax(ev, axis=-1)
    within = jnp.sum(jax.nn.one_hot(ei, EXPERTS_PER_GROUP) * ew[..., None], axis=1)
    comb = jax.nn.one_hot(gi[:, 0], N_EXPERT_GROUPS)[:, :, None] * gw[:, :, None] * within[:, None, :]
    comb = comb.reshape(-1, N_EXPERTS)
    hid = jax.nn.silu(jnp.einsum('td,edf->tef', t, p['w1'])) * jnp.einsum('td,edf->tef', t, p['w3'])
    hid = hid * comb[..., None].astype(hid.dtype)
    out = jnp.einsum('tef,efd->td', hid, p['w2'])
    return out.reshape(b, L, D)


def layer(x, mod, p, mix_fn):
    h = modulate(rmsnorm(x, p['norm1']), mod, 0)
    y, extra = mix_fn(h)
    x = x + mod[:, 2][:, None, :] * y
    h = modulate(rmsnorm(x, p['norm2']), mod, 1)
    x = x + mod[:, 5][:, None, :] * hier_moe(h, p)
    return x, extra


def setup_inputs(seed: int = 0) -> dict:
    key = jax.random.key(seed)
    keys = iter(jax.random.split(key, 48))
    nrm = lambda shape, s=1.0: jax.random.normal(next(keys), shape, jnp.float32) * s
    gain = lambda shape: 1.0 + nrm(shape, 0.02)
    dt0 = jnp.exp(jax.random.uniform(next(keys), (DEPTH, 2, SSD_HEADS)) * (jnp.log(0.1) - jnp.log(0.001)) + jnp.log(0.001))
    return {
        'x_prompt': nrm((BATCH, SEQ, D_MODEL)),
        'x_sample': nrm((DEC_BATCH, DEC_SEQ, D_MODEL)),
        'cache_k': nrm((DEC_BATCH, DEPTH, PAST_LEN, ATT_KV_HEADS, HEAD_DIM)),
        'cache_v': nrm((DEC_BATCH, DEPTH, PAST_LEN, ATT_KV_HEADS, HEAD_DIM)),
        'state_ssd': nrm((DEC_BATCH, DEPTH, 2, SSD_HEADS, SSD_HEADDIM, SSD_D_STATE), 0.1),
        'c': nrm((DEC_BATCH, D_MODEL)),
        'c_ctx': nrm((D_MODEL,)),
        'norm1_g': gain((DEPTH, D_MODEL)),
        'norm2_g': gain((DEPTH, D_MODEL)),
        'w_mod': nrm((DEPTH, D_MODEL, N_MOD * D_MODEL), 0.5 * D_MODEL ** -0.5),
        'b_mod': nrm((DEPTH, N_MOD * D_MODEL), 0.02),
        'w_in': nrm((DEPTH, D_MODEL, D_IN), D_MODEL ** -0.5),
        'ssd_conv_w': nrm((DEPTH, SSD_CONV, SSD_XBC), SSD_CONV ** -0.5),
        'ssd_conv_b': nrm((DEPTH, SSD_XBC), 0.02),
        'ssd_dt_bias': dt0 + jnp.log(-jnp.expm1(-dt0)),
        'ssd_A_log': jnp.log(jax.random.uniform(next(keys), (DEPTH, 2, SSD_HEADS), minval=1.0, maxval=16.0)),
        'ssd_D': 1.0 + nrm((DEPTH, SSD_HEADS), 0.1),
        'ssd_norm_g': gain((DEPTH, SSD_D_INNER)),
        'q_norm_g': gain((DEPTH, HEAD_DIM)),
        'k_norm_g': gain((DEPTH, HEAD_DIM)),
        'cf_dw_w': nrm((DEPTH, CONV_WIDTH, CONV_CH), CONV_WIDTH ** -0.5),
        'cf_dw_b': nrm((DEPTH, CONV_CH), 0.02),
        'cf_ln_g': gain((DEPTH, CONV_CH)),
        'cf_ln_b': nrm((DEPTH, CONV_CH), 0.02),
        'w_out': nrm((DEPTH, D_MIX, D_MODEL), D_MIX ** -0.5),
        'router_group_w': nrm((DEPTH, D_MODEL, N_EXPERT_GROUPS), D_MODEL ** -0.5),
        'router_group_b': nrm((DEPTH, N_EXPERT_GROUPS), 0.01),
        'router_expert_w': nrm((DEPTH, D_MODEL, N_EXPERTS), D_MODEL ** -0.5),
        'router_expert_b': nrm((DEPTH, N_EXPERTS), 0.01),
        'w1': nrm((DEPTH, N_EXPERTS, D_MODEL, EXPERT_FF), D_MODEL ** -0.5),
        'w3': nrm((DEPTH, N_EXPERTS, D_MODEL, EXPERT_FF), D_MODEL ** -0.5),
        'w2': nrm((DEPTH, N_EXPERTS, EXPERT_FF, D_MODEL), EXPERT_FF ** -0.5),
        'final_norm_g': gain((D_MODEL,)),
    }


def reference(x_prompt, x_sample, cache_k, cache_v, state_ssd, c, c_ctx, norm1_g, norm2_g, w_mod, b_mod,
              w_in, ssd_conv_w, ssd_conv_b, ssd_dt_bias, ssd_A_log, ssd_D, ssd_norm_g, q_norm_g, k_norm_g,
              cf_dw_w, cf_dw_b, cf_ln_g, cf_ln_b, w_out, router_group_w, router_group_b, router_expert_w,
              router_expert_b, w1, w3, w2, final_norm_g):
    rows = x_sample.shape[1] // GRID_W
    row = jnp.repeat(jnp.arange(rows, dtype=jnp.int32), GRID_W)
    col = jnp.tile(jnp.arange(GRID_W, dtype=jnp.int32), rows)
    h0_ctx = jnp.zeros((x_prompt.shape[0], 2, SSD_HEADS, SSD_HEADDIM, SSD_D_STATE), jnp.float32)
    xp, xs = x_prompt, x_sample
    ks, vs, ss = [], [], []
    for l in range(DEPTH):
        p = {'norm1': norm1_g[l], 'norm2': norm2_g[l], 'w_in': w_in[l],
             'ssd_conv_w': ssd_conv_w[l], 'ssd_conv_b': ssd_conv_b[l], 'ssd_dt_bias': ssd_dt_bias[l],
             'ssd_A_log': ssd_A_log[l], 'ssd_D': ssd_D[l], 'ssd_norm': ssd_norm_g[l],
             'q_norm': q_norm_g[l], 'k_norm': k_norm_g[l],
             'cf_dw_w': cf_dw_w[l], 'cf_dw_b': cf_dw_b[l], 'cf_ln_g': cf_ln_g[l], 'cf_ln_b': cf_ln_b[l],
             'w_out': w_out[l], 'rg_w': router_group_w[l], 'rg_b': router_group_b[l],
             're_w': router_expert_w[l], 're_b': router_expert_b[l],
             'w1': w1[l], 'w3': w3[l], 'w2': w2[l]}
        mod_c = adaln(c_ctx[None, :], w_mod[l], b_mod[l])
        xp, (k_c, v_c, st_c) = layer(xp, mod_c, p, lambda h: mixer(h, p, h0_ctx))
        ks.append(k_c)
        vs.append(v_c)
        ss.append(st_c)
        mod_x = adaln(c, w_mod[l], b_mod[l])
        xs, _ = layer(xs, mod_x, p, lambda h: mixer(h, p, state_ssd[:, l], pos=(row, col),
                                                      ctx_kv=(cache_k[:, l], cache_v[:, l])))
    y_prompt = rmsnorm(xp, final_norm_g)
    y_sample = rmsnorm(xs, final_norm_g)
    new_cache_k = jnp.stack(ks, axis=1)
    new_cache_v = jnp.stack(vs, axis=1)
    new_state_ssd = jnp.stack(ss, axis=1)
    return (y_prompt, y_sample, new_cache_k, new_cache_v, new_state_ssd)
```

```python
import functools

import numpy as np
import jax
import jax.numpy as jnp
from jax import lax
from jax.experimental import pallas as pl
from jax.experimental.pallas import tpu as pltpu

F32 = jnp.float32
BF16 = jnp.bfloat16
HIGHEST = lax.Precision.HIGHEST

D_MODEL = 2048
DEPTH = 4
GRID_W = 64
GROUP_W = 512
SSD_HEADS = 8
SSD_HEADDIM = 64
SSD_GROUPS = 2
SSD_D_STATE = 128
SSD_CHUNK = 128
HEAD_DIM = 64
ATT_HEADS = 8
ATT_KV_HEADS = 2
ROPE_THETA = 10000.0
ROPE_AXIS_DIM = 32
CONV_WIDTH = 31
CONV_HALO = 16
FNET_GROUP_CH = 128
N_EXPERT_GROUPS = 4
EXPERTS_PER_GROUP = 4
N_EXPERTS = 16
EXPERT_FF = 256
N_MOD = 6
EPS = 1e-6
LANES = 128

U_WIDTH = 4096
COL_XBC, COL_GLU, COL_Z, COL_Q, COL_FOUR, COL_K, COL_V, COL_DTF, COL_DTB = (
    0, 1024, 2048, 2560, 3072, 3584, 3712, 3840, 3968)
SRC_Z, SRC_XBC, SRC_DT, SRC_Q, SRC_K, SRC_V, SRC_GLU, SRC_FOUR, SRC_END = (
    0, 512, 1536, 1552, 2064, 2192, 2320, 3344, 3856)

VMEM_LIMIT = 48 * 1024 * 1024


def _params(*sem):
    return pltpu.CompilerParams(dimension_semantics=sem, vmem_limit_bytes=VMEM_LIMIT)


def _sigmoid(x):
    return 1.0 / (1.0 + jnp.exp(-x))


def _split_bf16(x):
    hi = x.astype(BF16)
    lo = (x - hi.astype(F32)).astype(BF16)
    return hi, lo


def _pow2_tile(pref, *dims):
    t = pref
    while any(d % t for d in dims):
        t //= 2
    return t


def _mod_kernel(c_ref, w_ref, b_ref, o_ref, acc_ref, *, nk):
    k = pl.program_id(2)

    @pl.when(k == 0)
    def _():
        acc_ref[...] = jnp.zeros_like(acc_ref)

    w = w_ref[0]
    tk, tn = w.shape
    for r in range(c_ref.shape[0]):
        c = c_ref[r]
        s = c * _sigmoid(c)
        acc_ref[r] += (s * w).reshape(tk // 8, 8, tn).sum(axis=0)

    @pl.when(k == nk - 1)
    def _():
        o_ref[0] = acc_ref[...].sum(axis=1) + b_ref[0]


def _adaln(cvec, w_mod, b_mod):
    r, d = cvec.shape
    depth, _, n = w_mod.shape
    tk, tn = 256, 2048
    nk = d // tk
    out = pl.pallas_call(
        functools.partial(_mod_kernel, nk=nk),
        grid=(depth, n // tn, nk),
        in_specs=[
            pl.BlockSpec((r, tk, 1), lambda l, j, k: (0, k, 0)),
            pl.BlockSpec((1, tk, tn), lambda l, j, k: (l, k, j)),
            pl.BlockSpec((1, 1, tn), lambda l, j, k: (l, 0, j)),
        ],
        out_specs=pl.BlockSpec((1, r, tn), lambda l, j, k: (l, 0, j)),
        out_shape=jax.ShapeDtypeStruct((depth, r, n), F32),
        scratch_shapes=[pltpu.VMEM((r, 8, tn), F32)],
        compiler_params=_params("parallel", "parallel", "arbitrary"),
        name="adaln",
    )(cvec.reshape(r, d, 1), w_mod, b_mod.reshape(depth, 1, n))
    return out.reshape(depth, r, N_MOD, d)


def _inproj_kernel(x_ref, g_ref, mod_ref, w_ref, o_ref, h_ref):
    @pl.when(pl.program_id(1) == 0)
    def _():
        x = x_ref[...]
        ms = jnp.mean(x * x, axis=-1, keepdims=True)
        y = x * lax.rsqrt(ms + EPS) * g_ref[...]
        m = mod_ref[0]
        h_ref[...] = (y * (1.0 + m[1:2]) + m[0:1]).astype(BF16)

    o_ref[...] = jnp.dot(h_ref[...], w_ref[...], preferred_element_type=F32)


def _inproj(x, g, mod, w, mod_row, tm):
    n, d = x.shape
    nout = w.shape[1]
    tn = 512
    return pl.pallas_call(
        _inproj_kernel,
        grid=(n // tm, nout // tn),
        in_specs=[
            pl.BlockSpec((tm, d), lambda i, j: (i, 0)),
            pl.BlockSpec((1, d), lambda i, j: (0, 0)),
            pl.BlockSpec((1, N_MOD, d), lambda i, j: (mod_row(i), 0, 0)),
            pl.BlockSpec((d, tn), lambda i, j: (0, j)),
        ],
        out_specs=pl.BlockSpec((tm, tn), lambda i, j: (i, j)),
        out_shape=jax.ShapeDtypeStruct((n, nout), F32),
        scratch_shapes=[pltpu.VMEM((tm, d), BF16)],
        compiler_params=_params("parallel", "arbitrary"),
        name="inproj",
    )(x, g.reshape(1, d), mod, w)


def _ssdpre_kernel(x_ref, w_ref, b_ref, o_ref, *, n_ctx_blocks, seq, dec_seq):
    x = x_ref[...]
    tl = x.shape[0]
    lseq = jnp.where(pl.program_id(0) < n_ctx_blocks, seq, dec_seq)
    pos = lax.broadcasted_iota(jnp.int32, (tl, 1), 0) & (lseq - 1)
    xp = jnp.where(pos == 0, 0.0, pltpu.roll(x, 1, 0))
    xn = jnp.where(pos == lseq - 1, 0.0, pltpu.roll(x, tl - 1, 0))
    w = w_ref[...]
    y = xp * w[0:1] + x * w[1:2] + xn * w[2:3] + b_ref[...]
    o_ref[...] = y * _sigmoid(y)


def _ssdpre(u, w, b, n_ctx, seq, dec_seq):
    n = u.shape[0]
    c = w.shape[1]
    tl = max(seq, dec_seq)
    assert n_ctx % tl == 0 and tl % seq == 0 and tl % dec_seq == 0
    tc = 256
    return pl.pallas_call(
        functools.partial(_ssdpre_kernel, n_ctx_blocks=n_ctx // tl, seq=seq, dec_seq=dec_seq),
        grid=(n // tl, c // tc),
        in_specs=[
            pl.BlockSpec((tl, tc), lambda i, j: (i, j + COL_XBC // tc)),
            pl.BlockSpec((3, tc), lambda i, j: (0, j)),
            pl.BlockSpec((1, tc), lambda i, j: (0, j)),
        ],
        out_specs=pl.BlockSpec((tl, tc), lambda i, j: (i, j)),
        out_shape=jax.ShapeDtypeStruct((n, c), F32),
        compiler_params=_params("parallel", "parallel"),
        name="ssd_conv",
    )(u, w, b.reshape(1, c))


def _ssd_kernel(x_ref, bc_ref, dt_ref, bias_ref, alog_ref, e_ref, h0_ref, y_ref, hout_ref, h_scr, *, nc):
    d = pl.program_id(1)
    c = pl.program_id(2)
    q = x_ref.shape[0]
    gw = SSD_HEADS // SSD_GROUPS * SSD_HEADDIM

    @pl.when(c == 0)
    def _():
        h_scr[...] = h0_ref[0, 0]

    x = x_ref[...]
    bc = bc_ref[...]
    raw = dt_ref[...] + bias_ref[0]
    dt = jnp.maximum(raw, 0.0) + jnp.log1p(jnp.exp(-jnp.abs(raw)))
    a = dt * (-jnp.exp(alog_ref[0]))
    ri = lax.broadcasted_iota(jnp.int32, (q, q), 0)
    ci = lax.broadcasted_iota(jnp.int32, (q, q), 1)
    tri = (ri >= ci).astype(F32)
    cs = jnp.dot(tri, a, precision=HIGHEST, preferred_element_type=F32)
    total = cs[q - 1:q, :]
    csa = jnp.where(d == 1, total - cs + a, cs)
    to_end = jnp.exp(total - csa)
    ecs = jnp.exp(csa)
    cdec = jnp.exp(total)
    lhs = jnp.concatenate([dt * to_end, ecs, jnp.broadcast_to(cdec, (8, LANES))], axis=0)
    hi, lo = _split_bf16(lhs)
    e = e_ref[...]
    ex = jnp.dot(hi, e, preferred_element_type=F32) + jnp.dot(lo, e, preferred_element_type=F32)
    w_x = ex[0:q]
    ecs_x = ex[q:2 * q]
    cd_x = ex[2 * q:2 * q + 1]
    csa_t = csa.T
    dt_t = dt.T
    sgn = jnp.where(d == 0, 1, -1)
    mask = (ri - ci) * sgn >= 0
    xb = x.astype(BF16)
    xw = (x * w_x).astype(BF16)
    lane_head = lax.broadcasted_iota(jnp.int32, (1, gw), 1) // SSD_HEADDIM
    for g in range(SSD_GROUPS):
        bm = bc[:, g * SSD_D_STATE:(g + 1) * SSD_D_STATE]
        cm = bc[:, (SSD_GROUPS + g) * SSD_D_STATE:(SSD_GROUPS + g + 1) * SSD_D_STATE].astype(BF16)
        bm_t = bm.T.astype(BF16)
        cb = jnp.dot(cm, bm_t, preferred_element_type=F32)
        xg = xb[:, g * gw:(g + 1) * gw]
        h_g = h_scr[:, g * gw:(g + 1) * gw]
        y_g = jnp.dot(cm, h_g.astype(BF16), preferred_element_type=F32) * ecs_x[:, g * gw:(g + 1) * gw]
        for hh in range(SSD_HEADS // SSD_GROUPS):
            h = g * (SSD_HEADS // SSD_GROUPS) + hh
            seg = csa[:, h:h + 1] - csa_t[h:h + 1, :]
            decay = jnp.exp(jnp.where(mask, seg, -jnp.inf))
            sc = (cb * decay * dt_t[h:h + 1, :]).astype(BF16)
            xm = jnp.where(lane_head == hh, xg, jnp.zeros_like(xg))
            y_g = y_g + jnp.dot(sc, xm, preferred_element_type=F32)
        st = jnp.dot(bm_t, xw[:, g * gw:(g + 1) * gw], preferred_element_type=F32)
        y_ref[0, :, g * gw:(g + 1) * gw] = y_g
        h_scr[:, g * gw:(g + 1) * gw] = cd_x[:, g * gw:(g + 1) * gw] * h_g + st

    @pl.when(c == nc - 1)
    def _():
        hout_ref[0, 0] = h_scr[...]


def _ssd_scan(xbc, u, bias, alog, expand, h0, row0, nb, lseq):
    q = SSD_CHUNK
    nc = lseq // q
    r0 = row0 // q

    def blk(b, d, c):
        return r0 + b * nc + jnp.where(d == 0, c, nc - 1 - c)

    y, hout = pl.pallas_call(
        functools.partial(_ssd_kernel, nc=nc),
        grid=(nb, 2, nc),
        in_specs=[
            pl.BlockSpec((q, 512), lambda b, d, c: (blk(b, d, c), 0)),
            pl.BlockSpec((q, 512), lambda b, d, c: (blk(b, d, c), 1)),
            pl.BlockSpec((q, LANES), lambda b, d, c: (blk(b, d, c), COL_DTF // LANES + d)),
            pl.BlockSpec((1, 1, LANES), lambda b, d, c: (d, 0, 0)),
            pl.BlockSpec((1, 1, LANES), lambda b, d, c: (d, 0, 0)),
            pl.BlockSpec((LANES, 512), lambda b, d, c: (0, 0)),
            pl.BlockSpec((1, 1, SSD_D_STATE, 512), lambda b, d, c: (b, d, 0, 0)),
        ],
        out_specs=[
            pl.BlockSpec((1, q, 512), lambda b, d, c: (d, blk(b, d, c) - r0, 0)),
            pl.BlockSpec((1, 1, SSD_D_STATE, 512), lambda b, d, c: (b, d, 0, 0)),
        ],
        out_shape=[
            jax.ShapeDtypeStruct((2, nb * lseq, 512), F32),
            jax.ShapeDtypeStruct((nb, 2, SSD_D_STATE, 512), F32),
        ],
        scratch_shapes=[pltpu.VMEM((SSD_D_STATE, 512), F32)],
        compiler_params=_params("parallel", "parallel", "arbitrary"),
        name="ssd_scan",
    )(xbc, xbc, u, bias, alog, expand, h0)
    return y, hout


def _gnorm_kernel(y_ref, x_ref, z_ref, dl_ref, g_ref, o_ref):
    y = y_ref[0] + y_ref[1] + x_ref[...] * dl_ref[...]
    z = z_ref[...]
    t = y * (z * _sigmoid(z))
    ms = jnp.mean(t * t, axis=-1, keepdims=True)
    o_ref[...] = (t * lax.rsqrt(ms + EPS) * g_ref[...]).astype(o_ref.dtype)


def _gnorm(y, xbc, u, d_lanes, g, tm):
    n = xbc.shape[0]
    return pl.pallas_call(
        _gnorm_kernel,
        grid=(n // tm,),
        in_specs=[
            pl.BlockSpec((2, tm, 512), lambda i: (0, i, 0)),
            pl.BlockSpec((tm, 512), lambda i: (i, 0)),
            pl.BlockSpec((tm, 512), lambda i: (i, COL_Z // 512)),
            pl.BlockSpec((1, 512), lambda i: (0, 0)),
            pl.BlockSpec((1, 512), lambda i: (0, 0)),
        ],
        out_specs=pl.BlockSpec((tm, 512), lambda i: (i, 0)),
        out_shape=jax.ShapeDtypeStruct((n, 512), BF16),
        compiler_params=_params("parallel"),
        name="ssd_gate_norm",
    )(y, xbc, u, d_lanes, g.reshape(1, 512))


def _head_rms(t, bd, gain):
    hi, lo = _split_bf16(t * t)
    ss = jnp.dot(hi, bd, preferred_element_type=F32) + jnp.dot(lo, bd, preferred_element_type=F32)
    return t * lax.rsqrt(ss * (1.0 / HEAD_DIM) + EPS) * gain


def _rope(t, cos, sin):
    w = t.shape[1]
    lane = lax.broadcasted_iota(jnp.int32, (1, w), 1)
    first = (lane & (ROPE_AXIS_DIM - 1)) < ROPE_AXIS_DIM // 2
    partner = jnp.where(first, pltpu.roll(t, w - ROPE_AXIS_DIM // 2, 1), pltpu.roll(t, ROPE_AXIS_DIM // 2, 1))
    return t * cos + partner * sin


def _qk_kernel(*refs, rope):
    if rope:
        q_ref, k_ref, gq_ref, gk_ref, bdq_ref, bdk_ref, cos_ref, sin_ref, cosk_ref, sink_ref, qo_ref, ko_ref = refs
    else:
        q_ref, k_ref, gq_ref, gk_ref, bdq_ref, bdk_ref, qo_ref, ko_ref = refs
    qn = _head_rms(q_ref[...], bdq_ref[...], gq_ref[...])
    kn = _head_rms(k_ref[...], bdk_ref[...], gk_ref[...])
    if rope:
        qn = _rope(qn, cos_ref[...], sin_ref[...])
        kn = _rope(kn, cosk_ref[...], sink_ref[...])
    qo_ref[...] = (qn * (HEAD_DIM ** -0.5)).astype(qo_ref.dtype)
    ko_ref[...] = kn


def _qk_prep(u, gq, gk, bdq, bdk, row0, nrows, tm, tables=None):
    kw = ATT_KV_HEADS * HEAD_DIM
    r0 = row0 // tm
    in_specs = [
        pl.BlockSpec((tm, 512), lambda i: (r0 + i, COL_Q // 512)),
        pl.BlockSpec((tm, kw), lambda i: (r0 + i, COL_K // kw)),
        pl.BlockSpec((1, 512), lambda i: (0, 0)),
        pl.BlockSpec((1, kw), lambda i: (0, 0)),
        pl.BlockSpec((512, 512), lambda i: (0, 0)),
        pl.BlockSpec((kw, kw), lambda i: (0, 0)),
    ]
    args = [u, u, gq, gk, bdq, bdk]
    if tables is not None:
        cos, sin = tables
        nb = cos.shape[0] // tm
        in_specs += [
            pl.BlockSpec((tm, 512), lambda i: (i % nb, 0)),
            pl.BlockSpec((tm, 512), lambda i: (i % nb, 0)),
            pl.BlockSpec((tm, kw), lambda i: (i % nb, 0)),
            pl.BlockSpec((tm, kw), lambda i: (i % nb, 0)),
        ]
        args += [cos, sin, cos, sin]
    return pl.pallas_call(
        functools.partial(_qk_kernel, rope=tables is not None),
        grid=(nrows // tm,),
        in_specs=in_specs,
        out_specs=[
            pl.BlockSpec((tm, 512), lambda i: (i, 0)),
            pl.BlockSpec((tm, kw), lambda i: (i, 0)),
        ],
        out_shape=[
            jax.ShapeDtypeStruct((nrows, 512), BF16),
            jax.ShapeDtypeStruct((nrows, kw), F32),
        ],
        compiler_params=_params("parallel"),
        name="qk_norm_rope" if tables is not None else "qk_norm",
    )(*args)


def _attn_kernel(q_ref, kt_ref, v_ref, o_ref):
    s = jnp.dot(q_ref[0, 0], kt_ref[0, 0], preferred_element_type=F32)
    m = jnp.max(s, axis=-1, keepdims=True)
    p = jnp.exp(s - m)
    l = jnp.sum(p, axis=-1, keepdims=True)
    o = jnp.dot(p.astype(BF16), v_ref[0, 0], preferred_element_type=F32)
    o_ref[0, 0] = (o / l).astype(o_ref.dtype)


def _attention(q, kt, v, tq):
    b, h, l, dh = q.shape
    s = kt.shape[3]
    g = h // kt.shape[1]
    return pl.pallas_call(
        _attn_kernel,
        grid=(b, h, l // tq),
        in_specs=[
            pl.BlockSpec((1, 1, tq, dh), lambda bi, hi, i: (bi, hi, i, 0)),
            pl.BlockSpec((1, 1, dh, s), lambda bi, hi, i: (bi, hi // g, 0, 0)),
            pl.BlockSpec((1, 1, s, dh), lambda bi, hi, i: (bi, hi // g, 0, 0)),
        ],
        out_specs=pl.BlockSpec((1, 1, tq, dh), lambda bi, hi, i: (bi, hi, i, 0)),
        out_shape=jax.ShapeDtypeStruct((b, h, l, dh), BF16),
        compiler_params=_params("parallel", "parallel", "parallel"),
        name="attention",
    )(q, kt, v)


def _glu(t):
    return t[:, :GROUP_W] * _sigmoid(t[:, GROUP_W:])


def _conf_kernel(m_ref, p_ref, n_ref, w_ref, b_ref, lg_ref, lb_ref, o_ref, scr, *, n_ctx_blocks, seq, dec_seq):
    i = pl.program_id(0)
    tl = m_ref.shape[0]
    lseq = jnp.where(i < n_ctx_blocks, seq, dec_seq)
    has_prev = ((i * tl) & (lseq - 1)) != 0
    has_next = (((i + 1) * tl) & (lseq - 1)) != 0
    scr[0:CONV_HALO, :] = jnp.where(has_prev, _glu(p_ref[...]), 0.0)
    scr[CONV_HALO:CONV_HALO + tl, :] = _glu(m_ref[...])
    scr[CONV_HALO + tl:2 * CONV_HALO + tl, :] = jnp.where(has_next, _glu(n_ref[...]), 0.0)
    acc = jnp.zeros((tl, GROUP_W), F32) + b_ref[...]
    off = CONV_HALO - CONV_WIDTH // 2
    for k in range(CONV_WIDTH):
        acc = acc + scr[off + k:off + k + tl, :] * w_ref[k:k + 1, :]
    mu = jnp.mean(acc, axis=-1, keepdims=True)
    xc = acc - mu
    var = jnp.mean(xc * xc, axis=-1, keepdims=True)
    y = xc * lax.rsqrt(var + EPS) * lg_ref[...] + lb_ref[...]
    o_ref[...] = (y * _sigmoid(y)).astype(o_ref.dtype)


def _conformer(u, w, b, lg, lb, n_ctx, seq, dec_seq):
    n = u.shape[0]
    tl = _pow2_tile(256, seq, dec_seq)
    hb = tl // CONV_HALO
    nhb = n // CONV_HALO
    gcol = COL_GLU // (2 * GROUP_W)
    return pl.pallas_call(
        functools.partial(_conf_kernel, n_ctx_blocks=n_ctx // tl, seq=seq, dec_seq=dec_seq),
        grid=(n // tl,),
        in_specs=[
            pl.BlockSpec((tl, 2 * GROUP_W), lambda i: (i, gcol)),
            pl.BlockSpec((CONV_HALO, 2 * GROUP_W), lambda i: (jnp.maximum(i * hb - 1, 0), gcol)),
            pl.BlockSpec((CONV_HALO, 2 * GROUP_W), lambda i: (jnp.minimum((i + 1) * hb, nhb - 1), gcol)),
            pl.BlockSpec((CONV_WIDTH, GROUP_W), lambda i: (0, 0)),
            pl.BlockSpec((1, GROUP_W), lambda i: (0, 0)),
            pl.BlockSpec((1, GROUP_W), lambda i: (0, 0)),
            pl.BlockSpec((1, GROUP_W), lambda i: (0, 0)),
        ],
        out_specs=pl.BlockSpec((tl, GROUP_W), lambda i: (i, 0)),
        out_shape=jax.ShapeDtypeStruct((n, GROUP_W), BF16),
        scratch_shapes=[pltpu.VMEM((tl + 2 * CONV_HALO, GROUP_W), F32)],
        compiler_params=_params("parallel"),
        name="conformer_conv",
    )(u, u, u, w, b.reshape(1, -1), lg.reshape(1, -1), lb.reshape(1, -1))


def _fnet_ch_kernel(x_ref, w_ref, o_ref):
    o_ref[...] = jnp.dot(x_ref[...].astype(BF16), w_ref[...], preferred_element_type=F32).astype(o_ref.dtype)


def _fnet_channels(u, wch, tm):
    n = u.shape[0]
    return pl.pallas_call(
        _fnet_ch_kernel,
        grid=(n // tm,),
        in_specs=[
            pl.BlockSpec((tm, 512), lambda i: (i, COL_FOUR // 512)),
            pl.BlockSpec((512, 1024), lambda i: (0, 0)),
        ],
        out_specs=pl.BlockSpec((tm, 1024), lambda i: (i, 0)),
        out_shape=jax.ShapeDtypeStruct((n, 1024), BF16),
        compiler_params=_params("parallel"),
        name="fnet_channels",
    )(u, wch)


def _fnet_pos_kernel(wc_ref, ws_ref, z_ref, o_ref):
    z = z_ref[...]
    o = jnp.dot(wc_ref[...], z[:, :512], preferred_element_type=F32)
    o = o + jnp.dot(ws_ref[...], z[:, 512:], preferred_element_type=F32)
    o_ref[...] = o.astype(o_ref.dtype)


def _fnet_positions(z, wc, ws, row0, nb, lseq):
    tm = min(256, lseq)
    r0 = row0 // lseq
    return pl.pallas_call(
        _fnet_pos_kernel,
        grid=(nb, lseq // tm),
        in_specs=[
            pl.BlockSpec((tm, lseq), lambda b, i: (i, 0)),
            pl.BlockSpec((tm, lseq), lambda b, i: (i, 0)),
            pl.BlockSpec((lseq, 1024), lambda b, i: (r0 + b, 0)),
        ],
        out_specs=pl.BlockSpec((tm, 512), lambda b, i: (b * (lseq // tm) + i, 0)),
        out_shape=jax.ShapeDtypeStruct((nb * lseq, 512), BF16),
        compiler_params=_params("parallel", "arbitrary"),
        name="fnet_positions",
    )(wc, ws, z)


def _dft_tables(n):
    k = jnp.arange(n, dtype=jnp.int32)
    ang = ((k[:, None] * k[None, :]) % n).astype(F32) * (2.0 * np.pi / n)
    scale = n ** -0.5
    return jnp.cos(ang) * scale, jnp.sin(ang) * scale


def _outproj_kernel(y1_ref, y2_ref, y3_ref, y4_ref, w_ref, x_ref, mod_ref, o_ref):
    acc = jnp.dot(y1_ref[...], w_ref[0], preferred_element_type=F32)
    acc = acc + jnp.dot(y2_ref[...], w_ref[1], preferred_element_type=F32)
    acc = acc + jnp.dot(y3_ref[...], w_ref[2], preferred_element_type=F32)
    acc = acc + jnp.dot(y4_ref[...], w_ref[3], preferred_element_type=F32)
    o_ref[...] = x_ref[...] + mod_ref[0][2:3] * acc


def _outproj(ys, w, x, mod, mod_row, tm):
    n, d = x.shape
    tn = 512
    yspec = pl.BlockSpec((tm, GROUP_W), lambda i, j: (i, 0))
    return pl.pallas_call(
        _outproj_kernel,
        grid=(n // tm, d // tn),
        in_specs=[yspec, yspec, yspec, yspec,
                  pl.BlockSpec((4, GROUP_W, tn), lambda i, j: (0, 0, j)),
                  pl.BlockSpec((tm, tn), lambda i, j: (i, j)),
                  pl.BlockSpec((1, N_MOD, tn), lambda i, j: (mod_row(i), 0, j))],
        out_specs=pl.BlockSpec((tm, tn), lambda i, j: (i, j)),
        out_shape=jax.ShapeDtypeStruct((n, d), F32),
        compiler_params=_params("parallel", "arbitrary"),
        name="outproj",
    )(*ys, w, x, mod)


ROUTE_LANE0 = N_EXPERT_GROUPS


def _route(lg):
    lane = lax.broadcasted_iota(jnp.int32, lg.shape, 1)
    lanef = lane.astype(F32)
    ninf = -jnp.inf
    isg = lane < N_EXPERT_GROUPS
    mg = jnp.max(jnp.where(isg, lg, ninf), axis=-1, keepdims=True)
    gsum = jnp.sum(jnp.where(isg, jnp.exp(jnp.where(isg, lg, ninf) - mg), 0.0), axis=-1, keepdims=True)
    gwt = 1.0 / gsum
    gi = jnp.min(jnp.where(isg, jnp.where(lg == mg, lanef, 1e9), 1e9), axis=-1, keepdims=True)
    grp = jnp.where(lane < ROUTE_LANE0 + N_EXPERTS, (lane - ROUTE_LANE0) >> 2, -1).astype(F32)
    el1 = jnp.where(grp == gi, lg, ninf)
    v1 = jnp.max(el1, axis=-1, keepdims=True)
    i1 = jnp.min(jnp.where(el1 == v1, lanef, 1e9), axis=-1, keepdims=True)
    el2 = jnp.where(lanef == i1, ninf, el1)
    v2 = jnp.max(el2, axis=-1, keepdims=True)
    i2 = jnp.min(jnp.where(el2 == v2, lanef, 1e9), axis=-1, keepdims=True)
    t = jnp.exp(v2 - v1)
    ew1 = 1.0 / (1.0 + t)
    ew2 = t * ew1
    return gwt * (jnp.where(lanef == i1, ew1, 0.0) + jnp.where(lanef == i2, ew2, 0.0))


def _moe_kernel(x_ref, g_ref, mod_ref, wr_ref, br_ref, w1_ref, w3_ref, w2_ref, o_ref, h_scr, comb_scr, acc_scr,
                *, ne):
    e = pl.program_id(1)

    @pl.when(e == 0)
    def _():
        x = x_ref[...]
        ms = jnp.mean(x * x, axis=-1, keepdims=True)
        y = x * lax.rsqrt(ms + EPS) * g_ref[...]
        m = mod_ref[0]
        h = y * (1.0 + m[4:5]) + m[3:4]
        hi, lo = _split_bf16(h)
        h_scr[...] = hi
        lg = (jnp.dot(hi, wr_ref[0], preferred_element_type=F32)
              + jnp.dot(lo, wr_ref[0], preferred_element_type=F32)
              + jnp.dot(hi, wr_ref[1], preferred_element_type=F32)) + br_ref[...]
        comb_scr[...] = _route(lg)
        acc_scr[...] = jnp.zeros_like(acc_scr)

    h = h_scr[...]
    a = jnp.dot(h, w1_ref[0], preferred_element_type=F32)
    b = jnp.dot(h, w3_ref[0], preferred_element_type=F32)
    lane = lax.broadcasted_iota(jnp.int32, (1, LANES), 1)
    ce = jnp.sum(jnp.where(lane == e + ROUTE_LANE0, comb_scr[...], 0.0), axis=-1, keepdims=True)
    hid = (a * _sigmoid(a)) * b * ce
    acc_scr[...] += jnp.dot(hid.astype(BF16), w2_ref[0], preferred_element_type=F32)

    @pl.when(e == ne - 1)
    def _():
        o_ref[...] = x_ref[...] + mod_ref[0][5:6] * acc_scr[...]


def _moe(x, g, mod, wr, br, w1, w3, w2, mod_row, tm):
    n, d = x.shape
    ne, _, ff = w1.shape
    return pl.pallas_call(
        functools.partial(_moe_kernel, ne=ne),
        grid=(n // tm, ne),
        in_specs=[
            pl.BlockSpec((tm, d), lambda i, e: (i, 0)),
            pl.BlockSpec((1, d), lambda i, e: (0, 0)),
            pl.BlockSpec((1, N_MOD, d), lambda i, e: (mod_row(i), 0, 0)),
            pl.BlockSpec((2, d, LANES), lambda i, e: (0, 0, 0)),
            pl.BlockSpec((1, LANES), lambda i, e: (0, 0)),
            pl.BlockSpec((1, d, ff), lambda i, e: (e, 0, 0)),
            pl.BlockSpec((1, d, ff), lambda i, e: (e, 0, 0)),
            pl.BlockSpec((1, ff, d), lambda i, e: (e, 0, 0)),
        ],
        out_specs=pl.BlockSpec((tm, d), lambda i, e: (i, 0)),
        out_shape=jax.ShapeDtypeStruct((n, d), F32),
        scratch_shapes=[pltpu.VMEM((tm, d), BF16), pltpu.VMEM((tm, LANES), F32), pltpu.VMEM((tm, d), F32)],
        compiler_params=_params("parallel", "arbitrary"),
        name="moe",
    )(x, g.reshape(1, d), mod, wr, br, w1, w3, w2)


def _fnorm_kernel(x_ref, g_ref, o_ref):
    x = x_ref[...]
    ms = jnp.mean(x * x, axis=-1, keepdims=True)
    o_ref[...] = x * lax.rsqrt(ms + EPS) * g_ref[...]


def _final_norm(x, g, row0, nrows, tm):
    d = x.shape[1]
    r0 = row0 // tm
    return pl.pallas_call(
        _fnorm_kernel,
        grid=(nrows // tm,),
        in_specs=[pl.BlockSpec((tm, d), lambda i: (r0 + i, 0)), pl.BlockSpec((1, d), lambda i: (0, 0))],
        out_specs=pl.BlockSpec((tm, d), lambda i: (i, 0)),
        out_shape=jax.ShapeDtypeStruct((nrows, d), F32),
        compiler_params=_params("parallel"),
        name="final_norm",
    )(x, g.reshape(1, d))


def _rope_tables(dec_seq):
    t = jnp.arange(dec_seq, dtype=jnp.int32)
    half = ROPE_AXIS_DIM // 2
    freqs = ROPE_THETA ** (-jnp.arange(half, dtype=F32) / half)

    def axis(pos):
        ang = pos.astype(F32)[:, None] * freqs[None, :]
        c, s = jnp.cos(ang), jnp.sin(ang)
        return jnp.concatenate([c, c], axis=-1), jnp.concatenate([-s, s], axis=-1)

    cr, sr = axis(t // GRID_W)
    cc, sc = axis(t % GRID_W)
    cos = jnp.tile(jnp.concatenate([cr, cc], axis=-1), (1, ATT_HEADS))
    sin = jnp.tile(jnp.concatenate([sr, sc], axis=-1), (1, ATT_HEADS))
    return cos, sin


def _block_diag_ones(width, block):
    i = np.arange(width) // block
    return jnp.asarray((i[:, None] == i[None, :]).astype(np.float32), dtype=BF16)


def _pack_w_in(w_in):
    depth, d, _ = w_in.shape
    seg = lambda a, b: w_in[:, :, a:b]
    zeros = lambda n: jnp.zeros((depth, d, n), w_in.dtype)
    parts = [seg(SRC_XBC, SRC_DT), seg(SRC_GLU, SRC_FOUR), seg(SRC_Z, SRC_XBC), seg(SRC_Q, SRC_K),
             seg(SRC_FOUR, SRC_END), seg(SRC_K, SRC_V), seg(SRC_V, SRC_GLU),
             seg(SRC_DT, SRC_DT + SSD_HEADS), zeros(LANES - SSD_HEADS),
             seg(SRC_DT + SSD_HEADS, SRC_Q), zeros(LANES - SSD_HEADS)]
    return jnp.concatenate(parts, axis=-1).astype(BF16)


def _lanes8(v):
    return jnp.pad(v.astype(F32), ((0, 0), (0, 0), (0, LANES - SSD_HEADS)))[:, :, None, :]


def kernel(x_prompt, x_sample, cache_k, cache_v, state_ssd, c, c_ctx, norm1_g, norm2_g, w_mod, b_mod, w_in,
           ssd_conv_w, ssd_conv_b, ssd_dt_bias, ssd_A_log, ssd_D, ssd_norm_g, q_norm_g, k_norm_g, cf_dw_w, cf_dw_b,
           cf_ln_g, cf_ln_b, w_out, router_group_w, router_group_b, router_expert_w, router_expert_b, w1, w3, w2,
           final_norm_g):
    bc, lc, d = x_prompt.shape
    bl, ll, _ = x_sample.shape
    past = cache_k.shape[2]
    n_ctx, n_lat = bc * lc, bl * ll
    n = n_ctx + n_lat
    depth = w_in.shape[0]
    kvw = ATT_KV_HEADS * HEAD_DIM
    assert lc & (lc - 1) == 0 and ll & (ll - 1) == 0 and lc % SSD_CHUNK == 0 and ll % SSD_CHUNK == 0
    assert n_ctx % ll == 0 and ll % GRID_W == 0

    tm = _pow2_tile(1024, n_ctx, ll)
    tm_moe = _pow2_tile(512, n_ctx, ll)

    def mod_row_fn(t):
        nb, per = n_ctx // t, ll // t
        return lambda i: jnp.where(i < nb, 0, 1 + (i - nb) // per)

    w_in_p = _pack_w_in(w_in)
    w_out_p = w_out.astype(BF16).reshape(depth, 4, GROUP_W, d)
    w1_b, w3_b, w2_b = w1.astype(BF16), w3.astype(BF16), w2.astype(BF16)
    wr = jnp.concatenate([router_group_w, router_expert_w,
                          jnp.zeros((depth, d, LANES - N_EXPERT_GROUPS - N_EXPERTS), F32)], axis=-1)
    wr_hi = wr.astype(BF16)
    wr_lo = (wr - wr_hi.astype(F32)).astype(BF16)
    wr_p = jnp.stack([wr_hi, wr_lo], axis=1)
    br_p = jnp.concatenate([router_group_b, router_expert_b,
                            jnp.zeros((depth, LANES - N_EXPERT_GROUPS - N_EXPERTS), F32)], axis=-1)[:, None, :]
    dt_bias_p = _lanes8(ssd_dt_bias)
    a_log_p = _lanes8(ssd_A_log)
    d_lanes = jnp.repeat(ssd_D.astype(F32), SSD_HEADDIM, axis=-1)[:, None, :]
    expand = jnp.asarray((np.arange(LANES)[:, None] == (np.arange(512)[None, :] // SSD_HEADDIM)), dtype=BF16)
    gq = jnp.tile(q_norm_g, (1, ATT_HEADS))[:, None, :]
    gk = jnp.tile(k_norm_g, (1, ATT_KV_HEADS))[:, None, :]
    bdq = _block_diag_ones(512, HEAD_DIM)
    bdk = _block_diag_ones(kvw, HEAD_DIM)
    rope_tabs = _rope_tables(ll)
    cch, sch = _dft_tables(FNET_GROUP_CH)
    eye4 = jnp.eye(4, dtype=F32)
    wch = jnp.concatenate([jnp.kron(eye4, cch), jnp.kron(eye4, sch)], axis=-1).astype(BF16)
    dft_c = {}
    for lseq in (lc, ll):
        cl, sl = _dft_tables(lseq)
        dft_c[lseq] = (cl.astype(BF16), (-sl).astype(BF16))

    mod_all = _adaln(jnp.concatenate([c_ctx[None, :], c], axis=0), w_mod, b_mod)

    x = jnp.concatenate([x_prompt.reshape(n_ctx, d), x_sample.reshape(n_lat, d)], axis=0)
    h0_ctx = jnp.zeros((bc, 2, SSD_D_STATE, 512), F32)
    ks, vs, ss = [], [], []
    for l in range(depth):
        mod = mod_all[l]
        u = _inproj(x, norm1_g[l], mod, w_in_p[l], mod_row_fn(tm), tm)

        xbc = _ssdpre(u, ssd_conv_w[l], ssd_conv_b[l], n_ctx, lc, ll)
        h0_lat = state_ssd[:, l].transpose(0, 1, 4, 2, 3).reshape(bl, 2, SSD_D_STATE, 512)
        y_c, hT_c = _ssd_scan(xbc, u, dt_bias_p[l], a_log_p[l], expand, h0_ctx, 0, bc, lc)
        y_l, _ = _ssd_scan(xbc, u, dt_bias_p[l], a_log_p[l], expand, h0_lat, n_ctx, bl, ll)
        y_ssd = _gnorm(jnp.concatenate([y_c, y_l], axis=1), xbc, u, d_lanes[l], ssd_norm_g[l], tm)
        ss.append(hT_c.reshape(bc, 2, SSD_D_STATE, SSD_HEADS, SSD_HEADDIM).transpose(0, 1, 3, 4, 2))

        q_c, k_c = _qk_prep(u, gq[l], gk[l], bdq, bdk, 0, n_ctx, tm)
        q_l, k_l = _qk_prep(u, gq[l], gk[l], bdq, bdk, n_ctx, n_lat, tm, tables=rope_tabs)
        v_c = u[:n_ctx, COL_V:COL_V + kvw]
        v_l = u[n_ctx:, COL_V:COL_V + kvw]
        ks.append(k_c.reshape(bc, lc, ATT_KV_HEADS, HEAD_DIM))
        vs.append(v_c.reshape(bc, lc, ATT_KV_HEADS, HEAD_DIM))

        def heads(t, b, s, nh):
            return t.reshape(b, s, nh, HEAD_DIM).transpose(0, 2, 1, 3)

        o_c = _attention(heads(q_c, bc, lc, ATT_HEADS),
                         heads(k_c, bc, lc, ATT_KV_HEADS).transpose(0, 1, 3, 2).astype(BF16),
                         heads(v_c, bc, lc, ATT_KV_HEADS).astype(BF16), min(256, lc))
        k_all = jnp.concatenate([cache_k[:, l], k_l.reshape(bl, ll, ATT_KV_HEADS, HEAD_DIM)], axis=1)
        v_all = jnp.concatenate([cache_v[:, l], v_l.reshape(bl, ll, ATT_KV_HEADS, HEAD_DIM)], axis=1)
        o_l = _attention(heads(q_l, bl, ll, ATT_HEADS),
                         k_all.transpose(0, 2, 3, 1).astype(BF16),
                         v_all.transpose(0, 2, 1, 3).astype(BF16), min(256, ll))
        y_att = jnp.concatenate([o_c.transpose(0, 2, 1, 3).reshape(n_ctx, GROUP_W),
                                 o_l.transpose(0, 2, 1, 3).reshape(n_lat, GROUP_W)], axis=0)

        y_conv = _conformer(u, cf_dw_w[l], cf_dw_b[l], cf_ln_g[l], cf_ln_b[l], n_ctx, lc, ll)

        zf = _fnet_channels(u, wch, tm)
        y_four = jnp.concatenate([_fnet_positions(zf, *dft_c[lc], 0, bc, lc),
                                  _fnet_positions(zf, *dft_c[ll], n_ctx, bl, ll)], axis=0)

        x = _outproj((y_ssd, y_att, y_conv, y_four), w_out_p[l], x, mod, mod_row_fn(tm), tm)
        x = _moe(x, norm2_g[l], mod, wr_p[l], br_p[l], w1_b[l], w3_b[l], w2_b[l], mod_row_fn(tm_moe), tm_moe)

    y_prompt = _final_norm(x, final_norm_g, 0, n_ctx, tm).reshape(bc, lc, d)
    y_sample = _final_norm(x, final_norm_g, n_ctx, n_lat, tm).reshape(bl, ll, d)
    return (y_prompt, y_sample, jnp.stack(ks, axis=1), jnp.stack(vs, axis=1), jnp.stack(ss, axis=1))
```

```python
import functools

import numpy as np
import jax
import jax.numpy as jnp
from jax import lax
from jax.experimental import pallas as pl
from jax.experimental.pallas import tpu as pltpu

F32 = jnp.float32
BF16 = jnp.bfloat16
HIGHEST = lax.Precision.HIGHEST

D_MODEL = 2048
DEPTH = 4
GRID_W = 64
GROUP_W = 512
SSD_HEADS = 8
SSD_HEADDIM = 64
SSD_GROUPS = 2
SSD_D_STATE = 128
SSD_CHUNK = 128
HEAD_DIM = 64
ATT_HEADS = 8
ATT_KV_HEADS = 2
ROPE_THETA = 10000.0
ROPE_AXIS_DIM = 32
CONV_WIDTH = 31
CONV_HALO = 16
SUBLANES = 8
FNET_GROUP_CH = 128
DFT_ROW_BLOCK = 64
N_EXPERT_GROUPS = 4
EXPERTS_PER_GROUP = 4
N_EXPERTS = 16
EXPERT_FF = 256
N_MOD = 6
EPS = 1e-6
LANES = 128

U_WIDTH = 4096
COL_XBC, COL_GLU, COL_Z, COL_Q, COL_FOUR, COL_K, COL_V, COL_DTF, COL_DTB = (
    0, 1024, 2048, 2560, 3072, 3584, 3712, 3840, 3968)
SRC_Z, SRC_XBC, SRC_DT, SRC_Q, SRC_K, SRC_V, SRC_GLU, SRC_FOUR, SRC_END = (
    0, 512, 1536, 1552, 2064, 2192, 2320, 3344, 3856)

VMEM_LIMIT = 48 * 1024 * 1024

NT_DIMS = (((1,), (1,)), ((), ()))


def _params(*sem):
    return pltpu.CompilerParams(dimension_semantics=sem, vmem_limit_bytes=VMEM_LIMIT)


def _sigmoid(x):
    return 1.0 / (1.0 + jnp.exp(-x))


def _split_bf16(x):
    hi = x.astype(BF16)
    lo = (x - hi.astype(F32)).astype(BF16)
    return hi, lo


def _pow2_tile(pref, *dims):
    t = pref
    while any(d % t for d in dims):
        t //= 2
    return t


def _mod_kernel(c_ref, w_ref, b_ref, o_ref, acc_ref, *, nk):
    k = pl.program_id(2)

    @pl.when(k == 0)
    def _():
        acc_ref[...] = jnp.zeros_like(acc_ref)

    w = w_ref[0]
    tk, tn = w.shape
    for r in range(c_ref.shape[0]):
        c = c_ref[r]
        s = c * _sigmoid(c)
        acc_ref[r] += (s * w).reshape(tk // SUBLANES, SUBLANES, tn).sum(axis=0)

    @pl.when(k == nk - 1)
    def _():
        o_ref[0] = acc_ref[...].sum(axis=1) + b_ref[0]


def _adaln(cvec, w_mod, b_mod):
    r, d = cvec.shape
    depth, _, n = w_mod.shape
    tk, tn = 256, 2048
    nk = d // tk
    out = pl.pallas_call(
        functools.partial(_mod_kernel, nk=nk),
        grid=(depth, n // tn, nk),
        in_specs=[
            pl.BlockSpec((r, tk, 1), lambda l, j, k: (0, k, 0)),
            pl.BlockSpec((1, tk, tn), lambda l, j, k: (l, k, j)),
            pl.BlockSpec((1, 1, tn), lambda l, j, k: (l, 0, j)),
        ],
        out_specs=pl.BlockSpec((1, r, tn), lambda l, j, k: (l, 0, j)),
        out_shape=jax.ShapeDtypeStruct((depth, r, n), F32),
        scratch_shapes=[pltpu.VMEM((r, SUBLANES, tn), F32)],
        compiler_params=_params("parallel", "parallel", "arbitrary"),
        name="adaln",
    )(cvec.reshape(r, d, 1), w_mod, b_mod.reshape(depth, 1, n))
    return out.reshape(depth, r, N_MOD, d)


def _inproj_kernel(x_ref, g_ref, mod_ref, w_ref, o_ref, h_ref):
    @pl.when(pl.program_id(1) == 0)
    def _():
        x = x_ref[...]
        ms = jnp.mean(x * x, axis=-1, keepdims=True)
        y = x * lax.rsqrt(ms + EPS) * g_ref[...]
        m = mod_ref[0]
        h_ref[...] = (y * (1.0 + m[1:2]) + m[0:1]).astype(BF16)

    o_ref[...] = jnp.dot(h_ref[...], w_ref[...], preferred_element_type=F32).astype(o_ref.dtype)


def _inproj(x, g, mod, w, mod_row, tm):
    n, d = x.shape
    nout = w.shape[1]
    tn = 1024
    return pl.pallas_call(
        _inproj_kernel,
        grid=(n // tm, nout // tn),
        in_specs=[
            pl.BlockSpec((tm, d), lambda i, j: (i, 0)),
            pl.BlockSpec((1, d), lambda i, j: (0, 0)),
            pl.BlockSpec((1, N_MOD, d), lambda i, j: (mod_row(i), 0, 0)),
            pl.BlockSpec((d, tn), lambda i, j: (0, j)),
        ],
        out_specs=pl.BlockSpec((tm, tn), lambda i, j: (i, j)),
        out_shape=jax.ShapeDtypeStruct((n, nout), BF16),
        scratch_shapes=[pltpu.VMEM((tm, d), BF16)],
        compiler_params=_params("parallel", "arbitrary"),
        name="inproj",
    )(x, g.reshape(1, d), mod, w)


def _ssdpre_kernel(x_ref, w_ref, b_ref, o_ref, *, n_ctx_blocks, seq, dec_seq):
    x = x_ref[...].astype(F32)
    tl = x.shape[0]
    lseq = jnp.where(pl.program_id(0) < n_ctx_blocks, seq, dec_seq)
    pos = lax.broadcasted_iota(jnp.int32, (tl, 1), 0) & (lseq - 1)
    xp = jnp.where(pos == 0, 0.0, pltpu.roll(x, 1, 0))
    xn = jnp.where(pos == lseq - 1, 0.0, pltpu.roll(x, tl - 1, 0))
    w = w_ref[...]
    y = xp * w[0:1] + x * w[1:2] + xn * w[2:3] + b_ref[...]
    o_ref[...] = (y * _sigmoid(y)).astype(o_ref.dtype)


def _ssdpre(u, w, b, n_ctx, seq, dec_seq):
    n = u.shape[0]
    c = w.shape[1]
    tl = max(seq, dec_seq)
    assert n_ctx % tl == 0 and tl % seq == 0 and tl % dec_seq == 0
    tc = 256
    return pl.pallas_call(
        functools.partial(_ssdpre_kernel, n_ctx_blocks=n_ctx // tl, seq=seq, dec_seq=dec_seq),
        grid=(n // tl, c // tc),
        in_specs=[
            pl.BlockSpec((tl, tc), lambda i, j: (i, j + COL_XBC // tc)),
            pl.BlockSpec((3, tc), lambda i, j: (0, j)),
            pl.BlockSpec((1, tc), lambda i, j: (0, j)),
        ],
        out_specs=pl.BlockSpec((tl, tc), lambda i, j: (i, j)),
        out_shape=jax.ShapeDtypeStruct((n, c), BF16),
        compiler_params=_params("parallel", "parallel"),
        name="ssd_conv",
    )(u, w, b.reshape(1, c))


def _ssd_kernel(x_ref, bc_ref, dt_ref, bias_ref, alog_ref, e_ref, et_ref, h0_ref, y_ref, hout_ref, h_scr, *, nc):
    d = pl.program_id(1)
    c = pl.program_id(2)
    q = x_ref.shape[0]
    hpg = SSD_HEADS // SSD_GROUPS
    gw = hpg * SSD_HEADDIM

    @pl.when(c == 0)
    def _():
        h_scr[...] = h0_ref[0, 0]

    x = x_ref[...].astype(F32)
    bc = bc_ref[...]
    raw = dt_ref[...].astype(F32) + bias_ref[0]
    dt = jnp.maximum(raw, 0.0) + jnp.log1p(jnp.exp(-jnp.abs(raw)))
    a = dt * (-jnp.exp(alog_ref[0]))
    ri = lax.broadcasted_iota(jnp.int32, (q, q), 0)
    ci = lax.broadcasted_iota(jnp.int32, (q, q), 1)
    tri = (ri >= ci).astype(F32)
    cs = jnp.dot(tri, a, precision=HIGHEST, preferred_element_type=F32)
    total = cs[q - 1:q, :]
    csa = jnp.where(d == 1, total - cs + a, cs)
    lhs = jnp.concatenate([dt * jnp.exp(total - csa), jnp.exp(csa)], axis=0)
    hi, lo = _split_bf16(lhs)
    e = e_ref[...]
    ex = jnp.dot(hi, e, preferred_element_type=F32) + jnp.dot(lo, e, preferred_element_type=F32)
    w_x = ex[0:q]
    ecs_x = ex[q:2 * q]
    cd_t = jnp.exp(jnp.broadcast_to(total, (LANES, LANES)).T)
    hi, lo = _split_bf16(cd_t)
    et = et_ref[...]
    cd_rows = jnp.dot(et, hi, preferred_element_type=F32) + jnp.dot(et, lo, preferred_element_type=F32)
    csa_t = csa.T
    dt_t = dt.T
    sgn = jnp.where(d == 0, 1, -1)
    mask = (ri - ci) * sgn >= 0
    xb = x_ref[...]
    xw = x * w_x
    lane_head = lax.broadcasted_iota(jnp.int32, (1, gw), 1) // SSD_HEADDIM
    for g in range(SSD_GROUPS):
        bm = bc[:, g * SSD_D_STATE:(g + 1) * SSD_D_STATE]
        cm = bc[:, (SSD_GROUPS + g) * SSD_D_STATE:(SSD_GROUPS + g + 1) * SSD_D_STATE]
        cb = lax.dot_general(cm, bm, NT_DIMS, preferred_element_type=F32)
        xg = xb[:, g * gw:(g + 1) * gw]
        h_g = h_scr[g * gw:(g + 1) * gw, :]
        y_g = lax.dot_general(cm, h_g.astype(BF16), NT_DIMS, preferred_element_type=F32)
        y_g = y_g * ecs_x[:, g * gw:(g + 1) * gw]
        for hh in range(hpg):
            h = g * hpg + hh
            seg = csa[:, h:h + 1] - csa_t[h:h + 1, :]
            decay = jnp.exp(jnp.where(mask, seg, -jnp.inf))
            sc = (cb * decay * dt_t[h:h + 1, :]).astype(BF16)
            xm = jnp.where(lane_head == hh, xg, jnp.zeros_like(xg))
            y_g = y_g + jnp.dot(sc, xm, preferred_element_type=F32)
        xw_t = xw[:, g * gw:(g + 1) * gw].T.astype(BF16)
        st = jnp.dot(xw_t, bm, preferred_element_type=F32)
        y_ref[0, :, g * gw:(g + 1) * gw] = y_g.astype(y_ref.dtype)
        h_scr[g * gw:(g + 1) * gw, :] = cd_rows[g * gw:(g + 1) * gw, :] * h_g + st

    @pl.when(c == nc - 1)
    def _():
        hout_ref[0, 0] = h_scr[...]


def _ssd_scan(xbc, u, bias, alog, expand, expand_t, h0, row0, nb, lseq):
    q = SSD_CHUNK
    nc = lseq // q
    r0 = row0 // q

    def blk(b, d, c):
        return r0 + b * nc + jnp.where(d == 0, c, nc - 1 - c)

    y, hout = pl.pallas_call(
        functools.partial(_ssd_kernel, nc=nc),
        grid=(nb, 2, nc),
        in_specs=[
            pl.BlockSpec((q, 512), lambda b, d, c: (blk(b, d, c), 0)),
            pl.BlockSpec((q, 512), lambda b, d, c: (blk(b, d, c), 1)),
            pl.BlockSpec((q, LANES), lambda b, d, c: (blk(b, d, c), COL_DTF // LANES + d)),
            pl.BlockSpec((1, 1, LANES), lambda b, d, c: (d, 0, 0)),
            pl.BlockSpec((1, 1, LANES), lambda b, d, c: (d, 0, 0)),
            pl.BlockSpec((LANES, 512), lambda b, d, c: (0, 0)),
            pl.BlockSpec((512, LANES), lambda b, d, c: (0, 0)),
            pl.BlockSpec((1, 1, 512, SSD_D_STATE), lambda b, d, c: (b, d, 0, 0)),
        ],
        out_specs=[
            pl.BlockSpec((1, q, 512), lambda b, d, c: (d, blk(b, d, c) - r0, 0)),
            pl.BlockSpec((1, 1, 512, SSD_D_STATE), lambda b, d, c: (b, d, 0, 0)),
        ],
        out_shape=[
            jax.ShapeDtypeStruct((2, nb * lseq, 512), BF16),
            jax.ShapeDtypeStruct((nb, 2, 512, SSD_D_STATE), F32),
        ],
        scratch_shapes=[pltpu.VMEM((512, SSD_D_STATE), F32)],
        compiler_params=_params("parallel", "parallel", "arbitrary"),
        name="ssd_scan",
    )(xbc, xbc, u, bias, alog, expand, expand_t, h0)
    return y, hout


def _gnorm_kernel(yc_ref, yl_ref, x_ref, z_ref, dl_ref, g_ref, o_ref, *, n_ctx_blocks):
    def body(y_ref):
        y = y_ref[0].astype(F32) + y_ref[1].astype(F32) + x_ref[...].astype(F32) * dl_ref[...]
        z = z_ref[...].astype(F32)
        t = y * (z * _sigmoid(z))
        ms = jnp.mean(t * t, axis=-1, keepdims=True)
        o_ref[...] = (t * lax.rsqrt(ms + EPS) * g_ref[...]).astype(o_ref.dtype)

    is_ctx = pl.program_id(0) < n_ctx_blocks
    pl.when(is_ctx)(lambda: body(yc_ref))
    pl.when(jnp.logical_not(is_ctx))(lambda: body(yl_ref))


def _gnorm(y_c, y_l, xbc, u, d_lanes, g, tm):
    n = xbc.shape[0]
    nbc = y_c.shape[1] // tm
    nbl = y_l.shape[1] // tm
    return pl.pallas_call(
        functools.partial(_gnorm_kernel, n_ctx_blocks=nbc),
        grid=(n // tm,),
        in_specs=[
            pl.BlockSpec((2, tm, 512), lambda i: (0, jnp.minimum(i, nbc - 1), 0)),
            pl.BlockSpec((2, tm, 512), lambda i: (0, jnp.clip(i - nbc, 0, nbl - 1), 0)),
            pl.BlockSpec((tm, 512), lambda i: (i, 0)),
            pl.BlockSpec((tm, 512), lambda i: (i, COL_Z // 512)),
            pl.BlockSpec((1, 512), lambda i: (0, 0)),
            pl.BlockSpec((1, 512), lambda i: (0, 0)),
        ],
        out_specs=pl.BlockSpec((tm, 512), lambda i: (i, 0)),
        out_shape=jax.ShapeDtypeStruct((n, 512), BF16),
        compiler_params=_params("parallel"),
        name="ssd_gate_norm",
    )(y_c, y_l, xbc, u, d_lanes, g.reshape(1, 512))


def _head_rms(t, bd, gain):
    hi, lo = _split_bf16(t * t)
    ss = jnp.dot(hi, bd, preferred_element_type=F32) + jnp.dot(lo, bd, preferred_element_type=F32)
    return t * lax.rsqrt(ss * (1.0 / HEAD_DIM) + EPS) * gain


def _rope(t, cos, sin):
    w = t.shape[1]
    lane = lax.broadcasted_iota(jnp.int32, (1, w), 1)
    first = (lane & (ROPE_AXIS_DIM - 1)) < ROPE_AXIS_DIM // 2
    partner = jnp.where(first, pltpu.roll(t, w - ROPE_AXIS_DIM // 2, 1), pltpu.roll(t, ROPE_AXIS_DIM // 2, 1))
    return t * cos + partner * sin


def _pad_variants(t, o_ref):
    lane = lax.broadcasted_iota(jnp.int32, (1, 2 * HEAD_DIM), 1)
    lo_half = lane < HEAD_DIM
    sw = pltpu.roll(t, HEAD_DIM, 1)
    o_ref[0, 0] = jnp.where(lo_half, t, 0.0).astype(o_ref.dtype)
    o_ref[0, 1] = jnp.where(lo_half, 0.0, sw).astype(o_ref.dtype)
    o_ref[0, 2] = jnp.where(lo_half, sw, 0.0).astype(o_ref.dtype)
    o_ref[0, 3] = jnp.where(lo_half, 0.0, t).astype(o_ref.dtype)


def _qk_kernel(*refs, rope, emit_kv):
    q_ref, k_ref, v_ref, gq_ref, gk_ref, bdq_ref, bdk_ref = refs[:7]
    refs = refs[7:]
    if rope:
        cos_ref, sin_ref, cosk_ref, sink_ref = refs[:4]
        refs = refs[4:]
    qo_ref, kz_ref, vz_ref = refs[:3]
    qn = _head_rms(q_ref[...].astype(F32), bdq_ref[...], gq_ref[...])
    kn = _head_rms(k_ref[...].astype(F32), bdk_ref[...], gk_ref[...])
    v = v_ref[...].astype(F32)
    if emit_kv:
        ko_ref, vo_ref = refs[3:5]
        ko_ref[...] = kn
        vo_ref[...] = v
    if rope:
        qn = _rope(qn, cos_ref[...], sin_ref[...])
        kn = _rope(kn, cosk_ref[...], sink_ref[...])
    qo_ref[...] = (qn * (HEAD_DIM ** -0.5)).astype(qo_ref.dtype)
    _pad_variants(kn, kz_ref)
    _pad_variants(v, vz_ref)


def _qk_prep(u, gq, gk, bdq, bdk, row0, nb, lseq, tm, tables=None, emit_kv=False):
    kw = ATT_KV_HEADS * HEAD_DIM
    nrows = nb * lseq
    r0 = row0 // tm
    per = lseq // tm
    in_specs = [
        pl.BlockSpec((tm, 512), lambda i: (r0 + i, COL_Q // 512)),
        pl.BlockSpec((tm, kw), lambda i: (r0 + i, COL_K // kw)),
        pl.BlockSpec((tm, kw), lambda i: (r0 + i, COL_V // kw)),
        pl.BlockSpec((1, 512), lambda i: (0, 0)),
        pl.BlockSpec((1, kw), lambda i: (0, 0)),
        pl.BlockSpec((512, 512), lambda i: (0, 0)),
        pl.BlockSpec((kw, kw), lambda i: (0, 0)),
    ]
    args = [u, u, u, gq, gk, bdq, bdk]
    if tables is not None:
        cos, sin = tables
        in_specs += [
            pl.BlockSpec((tm, 512), lambda i: (i % per, 0)),
            pl.BlockSpec((tm, 512), lambda i: (i % per, 0)),
            pl.BlockSpec((tm, kw), lambda i: (i % per, 0)),
            pl.BlockSpec((tm, kw), lambda i: (i % per, 0)),
        ]
        args += [cos, sin, cos, sin]
    zspec = pl.BlockSpec((1, 4, tm, kw), lambda i: (i // per, 0, i % per, 0))
    out_specs = [pl.BlockSpec((tm, 512), lambda i: (i, 0)), zspec, zspec]
    out_shape = [jax.ShapeDtypeStruct((nrows, 512), BF16),
                 jax.ShapeDtypeStruct((nb, 4, lseq, kw), BF16),
                 jax.ShapeDtypeStruct((nb, 4, lseq, kw), BF16)]
    if emit_kv:
        out_specs += [pl.BlockSpec((tm, kw), lambda i: (i, 0))] * 2
        out_shape += [jax.ShapeDtypeStruct((nrows, kw), F32)] * 2
    return pl.pallas_call(
        functools.partial(_qk_kernel, rope=tables is not None, emit_kv=emit_kv),
        grid=(nrows // tm,),
        in_specs=in_specs,
        out_specs=out_specs,
        out_shape=out_shape,
        compiler_params=_params("parallel"),
        name="qk_norm_rope" if tables is not None else "qk_norm",
    )(*args)


def _attn_kernel(*refs, cached):
    if cached:
        q_ref, kn_ref, vn_ref, kc_ref, vc_ref, o_ref = refs
    else:
        q_ref, kn_ref, vn_ref, o_ref = refs
    pair_w = 2 * HEAD_DIM
    for pair in range(ATT_HEADS // 2):
        kv = pair // (ATT_HEADS // ATT_KV_HEADS // 2)
        qp = q_ref[:, pair * pair_w:(pair + 1) * pair_w]
        acc = jnp.zeros((qp.shape[0], pair_w), F32)
        for par in range(2):
            z = 2 * kv + par
            s = lax.dot_general(qp, kn_ref[0, z], NT_DIMS, preferred_element_type=F32)
            m = jnp.max(s, axis=-1, keepdims=True)
            if cached:
                sc = lax.dot_general(qp, kc_ref[0, 0, z], NT_DIMS, preferred_element_type=F32)
                m = jnp.maximum(m, jnp.max(sc, axis=-1, keepdims=True))
                pc = jnp.exp(sc - m)
            p = jnp.exp(s - m)
            l = jnp.sum(p, axis=-1, keepdims=True)
            o = jnp.dot(p.astype(BF16), vn_ref[0, z], preferred_element_type=F32)
            if cached:
                l = l + jnp.sum(pc, axis=-1, keepdims=True)
                o = o + jnp.dot(pc.astype(BF16), vc_ref[0, 0, z], preferred_element_type=F32)
            acc = acc + o / l
        o_ref[:, pair * pair_w:(pair + 1) * pair_w] = acc.astype(o_ref.dtype)


def _attention(q, kz, vz, tq, cache=None):
    nb, _, l, kw = kz.shape
    per = l // tq
    in_specs = [
        pl.BlockSpec((tq, 512), lambda b, i: (b * per + i, 0)),
        pl.BlockSpec((1, 4, l, kw), lambda b, i: (b, 0, 0, 0)),
        pl.BlockSpec((1, 4, l, kw), lambda b, i: (b, 0, 0, 0)),
    ]
    args = [q, kz, vz]
    if cache is not None:
        kzc, vzc, layer = cache
        past = kzc.shape[3]
        cspec = pl.BlockSpec((1, 1, 4, past, kw), lambda b, i: (b, layer, 0, 0, 0))
        in_specs += [cspec, cspec]
        args += [kzc, vzc]
    return pl.pallas_call(
        functools.partial(_attn_kernel, cached=cache is not None),
        grid=(nb, per),
        in_specs=in_specs,
        out_specs=pl.BlockSpec((tq, 512), lambda b, i: (b * per + i, 0)),
        out_shape=jax.ShapeDtypeStruct((nb * l, 512), BF16),
        compiler_params=_params("parallel", "parallel"),
        name="attention_cached" if cache is not None else "attention",
    )(*args)


def _cache_variants(t):
    tb = t.astype(BF16)
    zero = jnp.zeros_like(tb[..., 0, :])
    var = [jnp.concatenate(pair, axis=-1) for pair in
           ((tb[..., 0, :], zero), (zero, tb[..., 0, :]), (tb[..., 1, :], zero), (zero, tb[..., 1, :]))]
    return jnp.stack(var, axis=2)


def _glu(t):
    t = t.astype(F32)
    return t[:, :GROUP_W] * _sigmoid(t[:, GROUP_W:])


def _conf_kernel(m_ref, p_ref, n_ref, w_ref, b_ref, lg_ref, lb_ref, o_ref, scr, sh_scr, *, n_ctx_blocks, seq,
                 dec_seq):
    i = pl.program_id(0)
    tl = m_ref.shape[0]
    rows = tl + 2 * CONV_HALO
    lseq = jnp.where(i < n_ctx_blocks, seq, dec_seq)
    has_prev = ((i * tl) & (lseq - 1)) != 0
    has_next = (((i + 1) * tl) & (lseq - 1)) != 0
    scr[0:CONV_HALO, :] = jnp.where(has_prev, _glu(p_ref[...]), 0.0)
    scr[CONV_HALO:CONV_HALO + tl, :] = _glu(m_ref[...])
    scr[CONV_HALO + tl:rows, :] = jnp.where(has_next, _glu(n_ref[...]), 0.0)
    full = scr[...]
    for b in range(1, SUBLANES):
        sh_scr[b - 1] = pltpu.roll(full, rows - b, 0)
    acc = jnp.zeros((tl, GROUP_W), F32) + b_ref[...]
    off = CONV_HALO - CONV_WIDTH // 2
    for k in range(CONV_WIDTH):
        a, b = divmod(off + k, SUBLANES)
        src = scr if b == 0 else sh_scr.at[b - 1]
        acc = acc + src[a * SUBLANES:a * SUBLANES + tl, :] * w_ref[k:k + 1, :]
    mu = jnp.mean(acc, axis=-1, keepdims=True)
    xc = acc - mu
    var = jnp.mean(xc * xc, axis=-1, keepdims=True)
    y = xc * lax.rsqrt(var + EPS) * lg_ref[...] + lb_ref[...]
    o_ref[...] = (y * _sigmoid(y)).astype(o_ref.dtype)


def _conformer(u, w, b, lg, lb, n_ctx, seq, dec_seq):
    n = u.shape[0]
    tl = _pow2_tile(256, seq, dec_seq)
    hb = tl // CONV_HALO
    nhb = n // CONV_HALO
    gcol = COL_GLU // (2 * GROUP_W)
    return pl.pallas_call(
        functools.partial(_conf_kernel, n_ctx_blocks=n_ctx // tl, seq=seq, dec_seq=dec_seq),
        grid=(n // tl,),
        in_specs=[
            pl.BlockSpec((tl, 2 * GROUP_W), lambda i: (i, gcol)),
            pl.BlockSpec((CONV_HALO, 2 * GROUP_W), lambda i: (jnp.maximum(i * hb - 1, 0), gcol)),
            pl.BlockSpec((CONV_HALO, 2 * GROUP_W), lambda i: (jnp.minimum((i + 1) * hb, nhb - 1), gcol)),
            pl.BlockSpec((CONV_WIDTH, GROUP_W), lambda i: (0, 0)),
            pl.BlockSpec((1, GROUP_W), lambda i: (0, 0)),
            pl.BlockSpec((1, GROUP_W), lambda i: (0, 0)),
            pl.BlockSpec((1, GROUP_W), lambda i: (0, 0)),
        ],
        out_specs=pl.BlockSpec((tl, GROUP_W), lambda i: (i, 0)),
        out_shape=jax.ShapeDtypeStruct((n, GROUP_W), BF16),
        scratch_shapes=[pltpu.VMEM((tl + 2 * CONV_HALO, GROUP_W), F32),
                        pltpu.VMEM((SUBLANES - 1, tl + 2 * CONV_HALO, GROUP_W), F32)],
        compiler_params=_params("parallel"),
        name="conformer_conv",
    )(u, u, u, w, b.reshape(1, -1), lg.reshape(1, -1), lb.reshape(1, -1))


def _fnet_ch_kernel(x_ref, w_ref, o_ref):
    o_ref[...] = jnp.dot(x_ref[...], w_ref[...], preferred_element_type=F32).astype(o_ref.dtype)


def _fnet_channels(u, wch, tm):
    n = u.shape[0]
    return pl.pallas_call(
        _fnet_ch_kernel,
        grid=(n // tm,),
        in_specs=[
            pl.BlockSpec((tm, 512), lambda i: (i, COL_FOUR // 512)),
            pl.BlockSpec((512, 1024), lambda i: (0, 0)),
        ],
        out_specs=pl.BlockSpec((tm, 1024), lambda i: (i, 0)),
        out_shape=jax.ShapeDtypeStruct((n, 1024), BF16),
        compiler_params=_params("parallel"),
        name="fnet_channels",
    )(u, wch)


def _fnet_pos_kernel(wc_ref, ws_ref, z_ref, o_ref):
    o = jnp.dot(wc_ref[...], z_ref[:, :512], preferred_element_type=F32)
    o = o + jnp.dot(ws_ref[...], z_ref[:, 512:], preferred_element_type=F32)
    o_ref[...] = o.astype(o_ref.dtype)


def _fnet_positions(z, wc, ws, row0, nb, lseq):
    tm = min(512, lseq)
    r0 = row0 // lseq
    return pl.pallas_call(
        _fnet_pos_kernel,
        grid=(nb, lseq // tm),
        in_specs=[
            pl.BlockSpec((tm, lseq), lambda b, i: (i, 0)),
            pl.BlockSpec((tm, lseq), lambda b, i: (i, 0)),
            pl.BlockSpec((lseq, 1024), lambda b, i: (r0 + b, 0)),
        ],
        out_specs=pl.BlockSpec((tm, 512), lambda b, i: (b * (lseq // tm) + i, 0)),
        out_shape=jax.ShapeDtypeStruct((nb * lseq, 512), BF16),
        compiler_params=_params("parallel", "arbitrary"),
        name="fnet_positions",
    )(wc, ws, z)


def _dft_tables(n):
    rb = DFT_ROW_BLOCK if n % DFT_ROW_BLOCK == 0 else 1
    t = jnp.arange(n, dtype=jnp.int32)[None, :]

    def trig(rows):
        ang = ((rows[:, None] * t) % n).astype(F32) * (2.0 * np.pi / n)
        return jnp.cos(ang), jnp.sin(ang)

    ca, sa = trig(jnp.arange(n // rb, dtype=jnp.int32) * rb)
    cb, sb = trig(jnp.arange(rb, dtype=jnp.int32))
    scale = n ** -0.5
    cos = (ca[:, None, :] * cb[None, :, :] - sa[:, None, :] * sb[None, :, :]).reshape(n, n) * scale
    sin = (sa[:, None, :] * cb[None, :, :] + ca[:, None, :] * sb[None, :, :]).reshape(n, n) * scale
    return cos, sin


def _outproj_kernel(ys_ref, yc_ref, ac_ref, al_ref, fc_ref, fl_ref, w_ref, x_ref, mod_ref, o_ref, *, n_ctx_blocks):
    def body(att_ref, four_ref):
        acc = jnp.dot(ys_ref[...], w_ref[0], preferred_element_type=F32)
        acc = acc + jnp.dot(att_ref[...], w_ref[1], preferred_element_type=F32)
        acc = acc + jnp.dot(yc_ref[...], w_ref[2], preferred_element_type=F32)
        acc = acc + jnp.dot(four_ref[...], w_ref[3], preferred_element_type=F32)
        o_ref[...] = x_ref[...] + mod_ref[0][2:3] * acc

    is_ctx = pl.program_id(0) < n_ctx_blocks
    pl.when(is_ctx)(lambda: body(ac_ref, fc_ref))
    pl.when(jnp.logical_not(is_ctx))(lambda: body(al_ref, fl_ref))


def _outproj(y_ssd, y_conv, att_c, att_l, four_c, four_l, w, x, mod, mod_row, tm):
    n, d = x.shape
    tn = 1024
    nbc = att_c.shape[0] // tm
    nbl = att_l.shape[0] // tm
    full = pl.BlockSpec((tm, GROUP_W), lambda i, j: (i, 0))
    ctx = pl.BlockSpec((tm, GROUP_W), lambda i, j: (jnp.minimum(i, nbc - 1), 0))
    lat = pl.BlockSpec((tm, GROUP_W), lambda i, j: (jnp.clip(i - nbc, 0, nbl - 1), 0))
    return pl.pallas_call(
        functools.partial(_outproj_kernel, n_ctx_blocks=nbc),
        grid=(n // tm, d // tn),
        in_specs=[full, full, ctx, lat, ctx, lat,
                  pl.BlockSpec((4, GROUP_W, tn), lambda i, j: (0, 0, j)),
                  pl.BlockSpec((tm, tn), lambda i, j: (i, j)),
                  pl.BlockSpec((1, N_MOD, tn), lambda i, j: (mod_row(i), 0, j))],
        out_specs=pl.BlockSpec((tm, tn), lambda i, j: (i, j)),
        out_shape=jax.ShapeDtypeStruct((n, d), F32),
        compiler_params=_params("parallel", "arbitrary"),
        name="outproj",
    )(y_ssd, y_conv, att_c, att_l, four_c, four_l, w, x, mod)


ROUTE_LANE0 = N_EXPERT_GROUPS


def _route(lg):
    lane = lax.broadcasted_iota(jnp.int32, lg.shape, 1)
    lanef = lane.astype(F32)
    ninf = -jnp.inf
    isg = lane < N_EXPERT_GROUPS
    mg = jnp.max(jnp.where(isg, lg, ninf), axis=-1, keepdims=True)
    gsum = jnp.sum(jnp.where(isg, jnp.exp(jnp.where(isg, lg, ninf) - mg), 0.0), axis=-1, keepdims=True)
    gwt = 1.0 / gsum
    gi = jnp.min(jnp.where(isg, jnp.where(lg == mg, lanef, 1e9), 1e9), axis=-1, keepdims=True)
    grp = jnp.where(lane < ROUTE_LANE0 + N_EXPERTS, (lane - ROUTE_LANE0) // EXPERTS_PER_GROUP, -1).astype(F32)
    el1 = jnp.where(grp == gi, lg, ninf)
    v1 = jnp.max(el1, axis=-1, keepdims=True)
    i1 = jnp.min(jnp.where(el1 == v1, lanef, 1e9), axis=-1, keepdims=True)
    el2 = jnp.where(lanef == i1, ninf, el1)
    v2 = jnp.max(el2, axis=-1, keepdims=True)
    i2 = jnp.min(jnp.where(el2 == v2, lanef, 1e9), axis=-1, keepdims=True)
    t = jnp.exp(v2 - v1)
    ew1 = 1.0 / (1.0 + t)
    ew2 = t * ew1
    return gwt * (jnp.where(lanef == i1, ew1, 0.0) + jnp.where(lanef == i2, ew2, 0.0))


def _moe_kernel(x_ref, g_ref, mod_ref, wr_ref, br_ref, w1_ref, w3_ref, w2_ref, o_ref, h_scr, comb_scr, acc_scr,
                *, ne):
    e = pl.program_id(1)

    @pl.when(e == 0)
    def _():
        x = x_ref[...]
        ms = jnp.mean(x * x, axis=-1, keepdims=True)
        y = x * lax.rsqrt(ms + EPS) * g_ref[...]
        m = mod_ref[0]
        h = y * (1.0 + m[4:5]) + m[3:4]
        hi, lo = _split_bf16(h)
        h_scr[...] = hi
        lg = (jnp.dot(hi, wr_ref[0], preferred_element_type=F32)
              + jnp.dot(lo, wr_ref[0], preferred_element_type=F32)
              + jnp.dot(hi, wr_ref[1], preferred_element_type=F32)) + br_ref[...]
        comb_scr[...] = _route(lg)
        acc_scr[...] = jnp.zeros_like(acc_scr)

    h = h_scr[...]
    a = jnp.dot(h, w1_ref[0], preferred_element_type=F32)
    b = jnp.dot(h, w3_ref[0], preferred_element_type=F32)
    lane = lax.broadcasted_iota(jnp.int32, (1, LANES), 1)
    ce = jnp.sum(jnp.where(lane == e + ROUTE_LANE0, comb_scr[...], 0.0), axis=-1, keepdims=True)
    hid = (a * _sigmoid(a)) * b * ce
    acc_scr[...] += jnp.dot(hid.astype(BF16), w2_ref[0], preferred_element_type=F32)

    @pl.when(e == ne - 1)
    def _():
        o_ref[...] = x_ref[...] + mod_ref[0][5:6] * acc_scr[...]


def _moe(x, g, mod, wr, br, w1, w3, w2, mod_row, tm):
    n, d = x.shape
    ne, _, ff = w1.shape
    return pl.pallas_call(
        functools.partial(_moe_kernel, ne=ne),
        grid=(n // tm, ne),
        in_specs=[
            pl.BlockSpec((tm, d), lambda i, e: (i, 0)),
            pl.BlockSpec((1, d), lambda i, e: (0, 0)),
            pl.BlockSpec((1, N_MOD, d), lambda i, e: (mod_row(i), 0, 0)),
            pl.BlockSpec((2, d, LANES), lambda i, e: (0, 0, 0)),
            pl.BlockSpec((1, LANES), lambda i, e: (0, 0)),
            pl.BlockSpec((1, d, ff), lambda i, e: (e, 0, 0)),
            pl.BlockSpec((1, d, ff), lambda i, e: (e, 0, 0)),
            pl.BlockSpec((1, ff, d), lambda i, e: (e, 0, 0)),
        ],
        out_specs=pl.BlockSpec((tm, d), lambda i, e: (i, 0)),
        out_shape=jax.ShapeDtypeStruct((n, d), F32),
        scratch_shapes=[pltpu.VMEM((tm, d), BF16), pltpu.VMEM((tm, LANES), F32), pltpu.VMEM((tm, d), F32)],
        compiler_params=_params("parallel", "arbitrary"),
        name="moe",
    )(x, g.reshape(1, d), mod, wr, br, w1, w3, w2)


def _fnorm_kernel(x_ref, g_ref, o_ref):
    x = x_ref[...]
    ms = jnp.mean(x * x, axis=-1, keepdims=True)
    o_ref[...] = x * lax.rsqrt(ms + EPS) * g_ref[...]


def _final_norm(x, g, row0, nrows, tm):
    d = x.shape[1]
    r0 = row0 // tm
    return pl.pallas_call(
        _fnorm_kernel,
        grid=(nrows // tm,),
        in_specs=[pl.BlockSpec((tm, d), lambda i: (r0 + i, 0)), pl.BlockSpec((1, d), lambda i: (0, 0))],
        out_specs=pl.BlockSpec((tm, d), lambda i: (i, 0)),
        out_shape=jax.ShapeDtypeStruct((nrows, d), F32),
        compiler_params=_params("parallel"),
        name="final_norm",
    )(x, g.reshape(1, d))


def _rope_tables(dec_seq):
    t = jnp.arange(dec_seq, dtype=jnp.int32)
    half = ROPE_AXIS_DIM // 2
    freqs = ROPE_THETA ** (-jnp.arange(half, dtype=F32) / half)

    def axis(pos):
        ang = pos.astype(F32)[:, None] * freqs[None, :]
        c, s = jnp.cos(ang), jnp.sin(ang)
        return jnp.concatenate([c, c], axis=-1), jnp.concatenate([-s, s], axis=-1)

    cr, sr = axis(t // GRID_W)
    cc, sc = axis(t % GRID_W)
    cos = jnp.tile(jnp.concatenate([cr, cc], axis=-1), (1, ATT_HEADS))
    sin = jnp.tile(jnp.concatenate([sr, sc], axis=-1), (1, ATT_HEADS))
    return cos, sin


def _block_diag_ones(width, block):
    i = np.arange(width) // block
    return jnp.asarray((i[:, None] == i[None, :]).astype(np.float32), dtype=BF16)


def _pack_w_in(w_in):
    depth, d, _ = w_in.shape
    seg = lambda a, b: w_in[:, :, a:b]
    zeros = lambda n: jnp.zeros((depth, d, n), w_in.dtype)
    parts = [seg(SRC_XBC, SRC_DT), seg(SRC_GLU, SRC_FOUR), seg(SRC_Z, SRC_XBC), seg(SRC_Q, SRC_K),
             seg(SRC_FOUR, SRC_END), seg(SRC_K, SRC_V), seg(SRC_V, SRC_GLU),
             seg(SRC_DT, SRC_DT + SSD_HEADS), zeros(LANES - SSD_HEADS),
             seg(SRC_DT + SSD_HEADS, SRC_Q), zeros(LANES - SSD_HEADS)]
    return jnp.concatenate(parts, axis=-1).astype(BF16)


def _lanes8(v):
    return jnp.pad(v.astype(F32), ((0, 0), (0, 0), (0, LANES - SSD_HEADS)))[:, :, None, :]


def kernel(x_prompt, x_sample, cache_k, cache_v, state_ssd, c, c_ctx, norm1_g, norm2_g, w_mod, b_mod, w_in,
           ssd_conv_w, ssd_conv_b, ssd_dt_bias, ssd_A_log, ssd_D, ssd_norm_g, q_norm_g, k_norm_g, cf_dw_w, cf_dw_b,
           cf_ln_g, cf_ln_b, w_out, router_group_w, router_group_b, router_expert_w, router_expert_b, w1, w3, w2,
           final_norm_g):
    bc, lc, d = x_prompt.shape
    bl, ll, _ = x_sample.shape
    n_ctx, n_lat = bc * lc, bl * ll
    n = n_ctx + n_lat
    depth = w_in.shape[0]
    kvw = ATT_KV_HEADS * HEAD_DIM
    assert lc & (lc - 1) == 0 and ll & (ll - 1) == 0 and lc % SSD_CHUNK == 0 and ll % SSD_CHUNK == 0
    assert n_ctx % ll == 0 and ll % GRID_W == 0

    tm = _pow2_tile(1024, n_ctx, ll)
    tm_moe = _pow2_tile(512, n_ctx, ll)
    tm_qk = _pow2_tile(512, lc, ll)

    def mod_row_fn(t):
        nb, per = n_ctx // t, ll // t
        return lambda i: jnp.where(i < nb, 0, 1 + (i - nb) // per)

    w_in_p = _pack_w_in(w_in)
    w_out_p = w_out.astype(BF16).reshape(depth, 4, GROUP_W, d)
    w1_b, w3_b, w2_b = w1.astype(BF16), w3.astype(BF16), w2.astype(BF16)
    wr = jnp.concatenate([router_group_w, router_expert_w,
                          jnp.zeros((depth, d, LANES - N_EXPERT_GROUPS - N_EXPERTS), F32)], axis=-1)
    wr_hi = wr.astype(BF16)
    wr_lo = (wr - wr_hi.astype(F32)).astype(BF16)
    wr_p = jnp.stack([wr_hi, wr_lo], axis=1)
    br_p = jnp.concatenate([router_group_b, router_expert_b,
                            jnp.zeros((depth, LANES - N_EXPERT_GROUPS - N_EXPERTS), F32)], axis=-1)[:, None, :]
    dt_bias_p = _lanes8(ssd_dt_bias)
    a_log_p = _lanes8(ssd_A_log)
    d_lanes = jnp.repeat(ssd_D.astype(F32), SSD_HEADDIM, axis=-1)[:, None, :]
    expand_np = np.arange(LANES)[:, None] == (np.arange(512)[None, :] // SSD_HEADDIM)
    expand = jnp.asarray(expand_np, dtype=BF16)
    expand_t = jnp.asarray(expand_np.T, dtype=BF16)
    gq = jnp.tile(q_norm_g, (1, ATT_HEADS))[:, None, :]
    gk = jnp.tile(k_norm_g, (1, ATT_KV_HEADS))[:, None, :]
    bdq = _block_diag_ones(512, HEAD_DIM)
    bdk = _block_diag_ones(kvw, HEAD_DIM)
    rope_tabs = _rope_tables(ll)
    cch, sch = _dft_tables(FNET_GROUP_CH)
    eye4 = jnp.eye(4, dtype=F32)
    wch = jnp.concatenate([jnp.kron(eye4, cch), jnp.kron(eye4, sch)], axis=-1).astype(BF16)
    dft_c = {}
    for lseq in (lc, ll):
        cl, sl = _dft_tables(lseq)
        dft_c[lseq] = (cl.astype(BF16), (-sl).astype(BF16))
    kz_cache = _cache_variants(cache_k)
    vz_cache = _cache_variants(cache_v)

    mod_all = _adaln(jnp.concatenate([c_ctx[None, :], c], axis=0), w_mod, b_mod)

    x = jnp.concatenate([x_prompt.reshape(n_ctx, d), x_sample.reshape(n_lat, d)], axis=0)
    h0_ctx = jnp.zeros((bc, 2, 512, SSD_D_STATE), F32)
    h0_lat = state_ssd.reshape(bl, depth, 2, 512, SSD_D_STATE)
    ks, vs, ss = [], [], []
    for l in range(depth):
        mod = mod_all[l]
        u = _inproj(x, norm1_g[l], mod, w_in_p[l], mod_row_fn(tm), tm)

        xbc = _ssdpre(u, ssd_conv_w[l], ssd_conv_b[l], n_ctx, lc, ll)
        y_c, h_c = _ssd_scan(xbc, u, dt_bias_p[l], a_log_p[l], expand, expand_t, h0_ctx, 0, bc, lc)
        y_l, _ = _ssd_scan(xbc, u, dt_bias_p[l], a_log_p[l], expand, expand_t, h0_lat[:, l], n_ctx, bl, ll)
        y_ssd = _gnorm(y_c, y_l, xbc, u, d_lanes[l], ssd_norm_g[l], tm)
        ss.append(h_c.reshape(bc, 2, SSD_HEADS, SSD_HEADDIM, SSD_D_STATE))

        q_c, kz_c, vz_c, k_c, v_c = _qk_prep(u, gq[l], gk[l], bdq, bdk, 0, bc, lc, tm_qk, emit_kv=True)
        q_l, kz_l, vz_l = _qk_prep(u, gq[l], gk[l], bdq, bdk, n_ctx, bl, ll, tm_qk, tables=rope_tabs)
        ks.append(k_c.reshape(bc, lc, ATT_KV_HEADS, HEAD_DIM))
        vs.append(v_c.reshape(bc, lc, ATT_KV_HEADS, HEAD_DIM))
        att_c = _attention(q_c, kz_c, vz_c, min(256, lc))
        att_l = _attention(q_l, kz_l, vz_l, min(256, ll), cache=(kz_cache, vz_cache, l))

        y_conv = _conformer(u, cf_dw_w[l], cf_dw_b[l], cf_ln_g[l], cf_ln_b[l], n_ctx, lc, ll)

        zf = _fnet_channels(u, wch, tm)
        four_c = _fnet_positions(zf, *dft_c[lc], 0, bc, lc)
        four_l = _fnet_positions(zf, *dft_c[ll], n_ctx, bl, ll)

        x = _outproj(y_ssd, y_conv, att_c, att_l, four_c, four_l, w_out_p[l], x, mod, mod_row_fn(tm), tm)
        x = _moe(x, norm2_g[l], mod, wr_p[l], br_p[l], w1_b[l], w3_b[l], w2_b[l], mod_row_fn(tm_moe), tm_moe)

    y_prompt = _final_norm(x, final_norm_g, 0, n_ctx, tm).reshape(bc, lc, d)
    y_sample = _final_norm(x, final_norm_g, n_ctx, n_lat, tm).reshape(bl, ll, d)
    return (y_prompt, y_sample, jnp.stack(ks, axis=1), jnp.stack(vs, axis=1), jnp.stack(ss, axis=1))
```

```python
import functools

import numpy as np
import jax
import jax.numpy as jnp
from jax import lax
from jax.experimental import pallas as pl
from jax.experimental.pallas import tpu as pltpu

F32 = jnp.float32
BF16 = jnp.bfloat16
HIGHEST = lax.Precision.HIGHEST

D_MODEL = 2048
DEPTH = 4
GRID_W = 64
GROUP_W = 512
SSD_HEADS = 8
SSD_HEADDIM = 64
SSD_GROUPS = 2
SSD_D_STATE = 128
SSD_CHUNK = 128
HEAD_DIM = 64
ATT_HEADS = 8
ATT_KV_HEADS = 2
ROPE_THETA = 10000.0
ROPE_AXIS_DIM = 32
CONV_WIDTH = 31
CONV_HALO = 16
SUBLANES = 8
FNET_GROUP_CH = 128
DFT_ROW_BLOCK = 64
N_EXPERT_GROUPS = 4
EXPERTS_PER_GROUP = 4
N_EXPERTS = 16
EXPERT_FF = 256
N_MOD = 6
EPS = 1e-6
LANES = 128

U_WIDTH = 4096
COL_XBC, COL_GLU, COL_Z, COL_Q, COL_FOUR, COL_K, COL_V, COL_DTF, COL_DTB = (
    0, 1024, 2048, 2560, 3072, 3584, 3712, 3840, 3968)
SRC_Z, SRC_XBC, SRC_DT, SRC_Q, SRC_K, SRC_V, SRC_GLU, SRC_FOUR, SRC_END = (
    0, 512, 1536, 1552, 2064, 2192, 2320, 3344, 3856)

VMEM_LIMIT = 48 * 1024 * 1024
VMEM_LIMIT_MOE = 54 * 1024 * 1024

NT_DIMS = (((1,), (1,)), ((), ()))
Q_SCALE = HEAD_DIM ** -0.5 * float(np.log2(np.e))
ATT_KEY_CHUNK = 1024
MOE_ROW_CHUNK = 256
SSD_SEQ_BLOCK = 2


def _params(*sem, vmem=VMEM_LIMIT):
    return pltpu.CompilerParams(dimension_semantics=sem, vmem_limit_bytes=vmem)


def _sigmoid(x):
    return 1.0 / (1.0 + jnp.exp(-x))


def _split_bf16(x):
    hi = x.astype(BF16)
    lo = (x - hi.astype(F32)).astype(BF16)
    return hi, lo


def _pow2_tile(pref, *dims):
    t = pref
    while any(d % t for d in dims):
        t //= 2
    return t


def _mod_kernel(c_ref, w_ref, b_ref, o_ref, acc_ref, *, nk):
    k = pl.program_id(2)

    @pl.when(k == 0)
    def _():
        acc_ref[...] = jnp.zeros_like(acc_ref)

    w = w_ref[0]
    tk, tn = w.shape
    for r in range(c_ref.shape[0]):
        c = c_ref[r]
        s = c * _sigmoid(c)
        acc_ref[r] += (s * w).reshape(tk // SUBLANES, SUBLANES, tn).sum(axis=0)

    @pl.when(k == nk - 1)
    def _():
        o_ref[0] = acc_ref[...].sum(axis=1) + b_ref[0]


def _adaln(cvec, w_mod, b_mod):
    r, d = cvec.shape
    depth, _, n = w_mod.shape
    tk, tn = 256, 2048
    nk = d // tk
    out = pl.pallas_call(
        functools.partial(_mod_kernel, nk=nk),
        grid=(depth, n // tn, nk),
        in_specs=[
            pl.BlockSpec((r, tk, 1), lambda l, j, k: (0, k, 0)),
            pl.BlockSpec((1, tk, tn), lambda l, j, k: (l, k, j)),
            pl.BlockSpec((1, 1, tn), lambda l, j, k: (l, 0, j)),
        ],
        out_specs=pl.BlockSpec((1, r, tn), lambda l, j, k: (l, 0, j)),
        out_shape=jax.ShapeDtypeStruct((depth, r, n), F32),
        scratch_shapes=[pltpu.VMEM((r, SUBLANES, tn), F32)],
        compiler_params=_params("parallel", "parallel", "arbitrary"),
        name="adaln",
    )(cvec.reshape(r, d, 1), w_mod, b_mod.reshape(depth, 1, n))
    return out.reshape(depth, r, N_MOD, d)


def _inproj_kernel(x_ref, g_ref, mod_ref, w_ref, o_ref):
    x = x_ref[...]
    ms = jnp.mean(x * x, axis=-1, keepdims=True)
    y = x * lax.rsqrt(ms + EPS) * g_ref[...]
    m = mod_ref[0, 0]
    h = (y * (1.0 + m[1:2]) + m[0:1]).astype(BF16)
    o_ref[...] = jnp.dot(h, w_ref[0], preferred_element_type=F32).astype(o_ref.dtype)


def _inproj(x, g, mod_all, w_all, layer, mod_row, tm):
    n, d = x.shape
    nout = w_all.shape[2]
    tn = 2048
    return pl.pallas_call(
        _inproj_kernel,
        grid=(n // tm, nout // tn),
        in_specs=[
            pl.BlockSpec((tm, d), lambda i, j: (i, 0)),
            pl.BlockSpec((1, d), lambda i, j: (0, 0)),
            pl.BlockSpec((1, 1, N_MOD, d), lambda i, j: (layer, mod_row(i), 0, 0)),
            pl.BlockSpec((1, d, tn), lambda i, j: (layer, 0, j)),
        ],
        out_specs=pl.BlockSpec((tm, tn), lambda i, j: (i, j)),
        out_shape=jax.ShapeDtypeStruct((n, nout), BF16),
        compiler_params=_params("parallel", "arbitrary"),
        name="inproj",
    )(x, g.reshape(1, d), mod_all, w_all)


def _ssdpre_kernel(x_ref, w_ref, b_ref, o_ref, *, n_ctx_blocks, seq, dec_seq):
    x = x_ref[...].astype(F32)
    tl = x.shape[0]
    lseq = jnp.where(pl.program_id(0) < n_ctx_blocks, seq, dec_seq)
    pos = lax.broadcasted_iota(jnp.int32, (tl, 1), 0) & (lseq - 1)
    xp = jnp.where(pos == 0, 0.0, pltpu.roll(x, 1, 0))
    xn = jnp.where(pos == lseq - 1, 0.0, pltpu.roll(x, tl - 1, 0))
    w = w_ref[...]
    y = xp * w[0:1] + x * w[1:2] + xn * w[2:3] + b_ref[...]
    o_ref[...] = (y * _sigmoid(y)).astype(o_ref.dtype)


def _ssdpre(u, w, b, n_ctx, seq, dec_seq):
    n = u.shape[0]
    c = w.shape[1]
    tl = max(seq, dec_seq)
    assert n_ctx % tl == 0 and tl % seq == 0 and tl % dec_seq == 0
    tc = 256
    return pl.pallas_call(
        functools.partial(_ssdpre_kernel, n_ctx_blocks=n_ctx // tl, seq=seq, dec_seq=dec_seq),
        grid=(n // tl, c // tc),
        in_specs=[
            pl.BlockSpec((tl, tc), lambda i, j: (i, j + COL_XBC // tc)),
            pl.BlockSpec((3, tc), lambda i, j: (0, j)),
            pl.BlockSpec((1, tc), lambda i, j: (0, j)),
        ],
        out_specs=pl.BlockSpec((tl, tc), lambda i, j: (i, j)),
        out_shape=jax.ShapeDtypeStruct((n, c), BF16),
        compiler_params=_params("parallel", "parallel"),
        name="ssd_conv",
    )(u, w, b.reshape(1, c))


def _ssd_direction(d, x_ref, bc_ref, dt_ref, bias, alog, e, et, h_scr, y_ref):
    q = x_ref.shape[0]
    hpg = SSD_HEADS // SSD_GROUPS
    gw = hpg * SSD_HEADDIM
    x = x_ref[...].astype(F32)
    bc = bc_ref[...]
    raw = dt_ref[...].astype(F32) + bias
    dt = jnp.maximum(raw, 0.0) + jnp.log1p(jnp.exp(-jnp.abs(raw)))
    a = dt * (-jnp.exp(alog))
    ri = lax.broadcasted_iota(jnp.int32, (q, q), 0)
    ci = lax.broadcasted_iota(jnp.int32, (q, q), 1)
    tri = (ri >= ci).astype(F32)
    cs = jnp.dot(tri, a, precision=HIGHEST, preferred_element_type=F32)
    total = cs[q - 1:q, :]
    csa = cs if d == 0 else total - cs + a
    mask = (ri >= ci) if d == 0 else (ri <= ci)
    lhs = jnp.concatenate([dt * jnp.exp(total - csa), jnp.exp(csa)], axis=0)
    hi, lo = _split_bf16(lhs)
    ex = jnp.dot(hi, e, preferred_element_type=F32) + jnp.dot(lo, e, preferred_element_type=F32)
    w_x = ex[0:q]
    ecs_x = ex[q:2 * q]
    cd_t = jnp.exp(jnp.broadcast_to(total, (LANES, LANES)).T)
    hi, lo = _split_bf16(cd_t)
    cd_rows = jnp.dot(et, hi, preferred_element_type=F32) + jnp.dot(et, lo, preferred_element_type=F32)
    csa_t = csa.T
    dt_t = dt.T
    xb = x_ref[...]
    xw = x * w_x
    lane_head = lax.broadcasted_iota(jnp.int32, (1, gw), 1) // SSD_HEADDIM
    for g in range(SSD_GROUPS):
        bm = bc[:, g * SSD_D_STATE:(g + 1) * SSD_D_STATE]
        cm = bc[:, (SSD_GROUPS + g) * SSD_D_STATE:(SSD_GROUPS + g + 1) * SSD_D_STATE]
        cb = lax.dot_general(cm, bm, NT_DIMS, preferred_element_type=F32)
        xg = xb[:, g * gw:(g + 1) * gw]
        h_g = h_scr[g * gw:(g + 1) * gw, :]
        y_g = lax.dot_general(cm, h_g.astype(BF16), NT_DIMS, preferred_element_type=F32)
        y_g = y_g * ecs_x[:, g * gw:(g + 1) * gw]
        for hh in range(hpg):
            h = g * hpg + hh
            seg = csa[:, h:h + 1] - csa_t[h:h + 1, :]
            decay = jnp.exp(jnp.where(mask, seg, -jnp.inf))
            sc = (cb * decay * dt_t[h:h + 1, :]).astype(BF16)
            xm = jnp.where(lane_head == hh, xg, jnp.zeros_like(xg))
            y_g = y_g + jnp.dot(sc, xm, preferred_element_type=F32)
        xw_t = xw[:, g * gw:(g + 1) * gw].T.astype(BF16)
        st = jnp.dot(xw_t, bm, preferred_element_type=F32)
        y_ref[:, g * gw:(g + 1) * gw] = y_g.astype(y_ref.dtype)
        h_scr[g * gw:(g + 1) * gw, :] = cd_rows[g * gw:(g + 1) * gw, :] * h_g + st


def _ssd_kernel(xf_ref, bcf_ref, dtf_ref, xb_ref, bcb_ref, dtb_ref, bias_ref, alog_ref, e_ref, et_ref, h0_ref,
                yf_ref, yb_ref, hout_ref, h_scr, *, nc):
    c = pl.program_id(1)

    @pl.when(c == 0)
    def _():
        h_scr[...] = h0_ref[...]

    e = e_ref[...]
    et = et_ref[...]
    for s in range(xf_ref.shape[0]):
        _ssd_direction(0, xf_ref.at[s], bcf_ref.at[s], dtf_ref.at[s], bias_ref[0], alog_ref[0], e, et,
                       h_scr.at[s, 0], yf_ref.at[s])
        _ssd_direction(1, xb_ref.at[s], bcb_ref.at[s], dtb_ref.at[s], bias_ref[1], alog_ref[1], e, et,
                       h_scr.at[s, 1], yb_ref.at[s])

    @pl.when(c == nc - 1)
    def _():
        hout_ref[...] = h_scr[...]


def _ssd_scan(xbc, u, bias, alog, expand, expand_t, h0, row0, nb, lseq):
    q = SSD_CHUNK
    nc = lseq // q
    n = xbc.shape[0]
    s0 = row0 // lseq
    sb = SSD_SEQ_BLOCK if (nb % SSD_SEQ_BLOCK == 0 and s0 % SSD_SEQ_BLOCK == 0) else 1
    xbc3 = xbc.reshape(n // lseq, lseq, xbc.shape[1])
    u3 = u.reshape(n // lseq, lseq, u.shape[1])
    sq = lambda b: s0 // sb + b
    sspec = pl.BlockSpec((sb, 2, 512, SSD_D_STATE), lambda b, c: (b, 0, 0, 0))
    yf, yb, hout = pl.pallas_call(
        functools.partial(_ssd_kernel, nc=nc),
        grid=(nb // sb, nc),
        in_specs=[
            pl.BlockSpec((sb, q, 512), lambda b, c: (sq(b), c, 0)),
            pl.BlockSpec((sb, q, 512), lambda b, c: (sq(b), c, 1)),
            pl.BlockSpec((sb, q, LANES), lambda b, c: (sq(b), c, COL_DTF // LANES)),
            pl.BlockSpec((sb, q, 512), lambda b, c: (sq(b), nc - 1 - c, 0)),
            pl.BlockSpec((sb, q, 512), lambda b, c: (sq(b), nc - 1 - c, 1)),
            pl.BlockSpec((sb, q, LANES), lambda b, c: (sq(b), nc - 1 - c, COL_DTB // LANES)),
            pl.BlockSpec((2, 1, LANES), lambda b, c: (0, 0, 0)),
            pl.BlockSpec((2, 1, LANES), lambda b, c: (0, 0, 0)),
            pl.BlockSpec((LANES, 512), lambda b, c: (0, 0)),
            pl.BlockSpec((512, LANES), lambda b, c: (0, 0)),
            sspec,
        ],
        out_specs=[
            pl.BlockSpec((sb, q, 512), lambda b, c: (b, c, 0)),
            pl.BlockSpec((sb, q, 512), lambda b, c: (b, nc - 1 - c, 0)),
            sspec,
        ],
        out_shape=[
            jax.ShapeDtypeStruct((nb, lseq, 512), BF16),
            jax.ShapeDtypeStruct((nb, lseq, 512), BF16),
            jax.ShapeDtypeStruct((nb, 2, 512, SSD_D_STATE), F32),
        ],
        scratch_shapes=[pltpu.VMEM((sb, 2, 512, SSD_D_STATE), F32)],
        compiler_params=_params("parallel", "arbitrary"),
        name="ssd_scan",
    )(xbc3, xbc3, u3, xbc3, xbc3, u3, bias, alog, expand, expand_t, h0)
    return yf.reshape(nb * lseq, 512), yb.reshape(nb * lseq, 512), hout


def _gnorm_kernel(yfc_ref, ybc_ref, yfl_ref, ybl_ref, x_ref, z_ref, dl_ref, g_ref, o_ref, *, n_ctx_blocks):
    def body(yf_ref, yb_ref):
        y = yf_ref[...].astype(F32) + yb_ref[...].astype(F32) + x_ref[...].astype(F32) * dl_ref[...]
        z = z_ref[...].astype(F32)
        t = y * (z * _sigmoid(z))
        ms = jnp.mean(t * t, axis=-1, keepdims=True)
        o_ref[...] = (t * lax.rsqrt(ms + EPS) * g_ref[...]).astype(o_ref.dtype)

    is_ctx = pl.program_id(0) < n_ctx_blocks
    pl.when(is_ctx)(lambda: body(yfc_ref, ybc_ref))
    pl.when(jnp.logical_not(is_ctx))(lambda: body(yfl_ref, ybl_ref))


def _gnorm(y_c, y_l, xbc, u, d_lanes, g, tm):
    n = xbc.shape[0]
    nbc = y_c[0].shape[0] // tm
    nbl = y_l[0].shape[0] // tm
    ctx = pl.BlockSpec((tm, 512), lambda i: (jnp.minimum(i, nbc - 1), 0))
    lat = pl.BlockSpec((tm, 512), lambda i: (jnp.clip(i - nbc, 0, nbl - 1), 0))
    return pl.pallas_call(
        functools.partial(_gnorm_kernel, n_ctx_blocks=nbc),
        grid=(n // tm,),
        in_specs=[
            ctx, ctx, lat, lat,
            pl.BlockSpec((tm, 512), lambda i: (i, 0)),
            pl.BlockSpec((tm, 512), lambda i: (i, COL_Z // 512)),
            pl.BlockSpec((1, 512), lambda i: (0, 0)),
            pl.BlockSpec((1, 512), lambda i: (0, 0)),
        ],
        out_specs=pl.BlockSpec((tm, 512), lambda i: (i, 0)),
        out_shape=jax.ShapeDtypeStruct((n, 512), BF16),
        compiler_params=_params("parallel"),
        name="ssd_gate_norm",
    )(*y_c, *y_l, xbc, u, d_lanes, g.reshape(1, 512))


def _head_rms(t, bd, gain):
    hi, lo = _split_bf16(t * t)
    ss = jnp.dot(hi, bd, preferred_element_type=F32) + jnp.dot(lo, bd, preferred_element_type=F32)
    return t * lax.rsqrt(ss * (1.0 / HEAD_DIM) + EPS) * gain


def _rope(t, cos, sin):
    w = t.shape[1]
    lane = lax.broadcasted_iota(jnp.int32, (1, w), 1)
    first = (lane & (ROPE_AXIS_DIM - 1)) < ROPE_AXIS_DIM // 2
    partner = jnp.where(first, pltpu.roll(t, w - ROPE_AXIS_DIM // 2, 1), pltpu.roll(t, ROPE_AXIS_DIM // 2, 1))
    return t * cos + partner * sin


def _dup_variants(t, o_ref, ones_tail):
    w = 2 * HEAD_DIM
    lane = lax.broadcasted_iota(jnp.int32, (1, w), 1)
    lo_half = lane < HEAD_DIM
    sw = pltpu.roll(t, HEAD_DIM, 1)
    o_ref[0, 0, :, 0:w] = jnp.where(lo_half, t, sw).astype(o_ref.dtype)
    o_ref[0, 1, :, 0:w] = jnp.where(lo_half, sw, t).astype(o_ref.dtype)
    if ones_tail:
        ones = jnp.ones(t.shape, o_ref.dtype)
        o_ref[0, 0, :, w:2 * w] = ones
        o_ref[0, 1, :, w:2 * w] = ones


def _qk_kernel(*refs, rope, emit_kv):
    q_ref, k_ref, v_ref, gq_ref, gk_ref, bdq_ref, bdk_ref = refs[:7]
    refs = refs[7:]
    if rope:
        cos_ref, sin_ref, cosk_ref, sink_ref = refs[:4]
        refs = refs[4:]
    qo_ref, kz_ref, vz_ref = refs[:3]
    qn = _head_rms(q_ref[...].astype(F32), bdq_ref[...], gq_ref[...])
    kn = _head_rms(k_ref[...].astype(F32), bdk_ref[...], gk_ref[...])
    v = v_ref[...].astype(F32)
    if emit_kv:
        ko_ref, vo_ref = refs[3:5]
        ko_ref[...] = kn
        vo_ref[...] = v
    if rope:
        qn = _rope(qn, cos_ref[...], sin_ref[...])
        kn = _rope(kn, cosk_ref[...], sink_ref[...])
    qo_ref[...] = (qn * Q_SCALE).astype(qo_ref.dtype)
    _dup_variants(kn, kz_ref, ones_tail=False)
    _dup_variants(v, vz_ref, ones_tail=True)


def _qk_prep(u, gq, gk, bdq, bdk, row0, nb, lseq, tm, tables=None, emit_kv=False):
    kw = ATT_KV_HEADS * HEAD_DIM
    nrows = nb * lseq
    r0 = row0 // tm
    per = lseq // tm
    in_specs = [
        pl.BlockSpec((tm, 512), lambda i: (r0 + i, COL_Q // 512)),
        pl.BlockSpec((tm, kw), lambda i: (r0 + i, COL_K // kw)),
        pl.BlockSpec((tm, kw), lambda i: (r0 + i, COL_V // kw)),
        pl.BlockSpec((1, 512), lambda i: (0, 0)),
        pl.BlockSpec((1, kw), lambda i: (0, 0)),
        pl.BlockSpec((512, 512), lambda i: (0, 0)),
        pl.BlockSpec((kw, kw), lambda i: (0, 0)),
    ]
    args = [u, u, u, gq, gk, bdq, bdk]
    if tables is not None:
        cos, sin = tables
        in_specs += [
            pl.BlockSpec((tm, 512), lambda i: (i % per, 0)),
            pl.BlockSpec((tm, 512), lambda i: (i % per, 0)),
            pl.BlockSpec((tm, kw), lambda i: (i % per, 0)),
            pl.BlockSpec((tm, kw), lambda i: (i % per, 0)),
        ]
        args += [cos, sin, cos, sin]
    zspec = pl.BlockSpec((1, ATT_KV_HEADS, tm, kw), lambda i: (i // per, 0, i % per, 0))
    out_specs = [pl.BlockSpec((tm, 512), lambda i: (i, 0)), zspec, zspec]
    vspec = pl.BlockSpec((1, ATT_KV_HEADS, tm, 2 * kw), lambda i: (i // per, 0, i % per, 0))
    out_specs[2] = vspec
    out_shape = [jax.ShapeDtypeStruct((nrows, 512), BF16),
                 jax.ShapeDtypeStruct((nb, ATT_KV_HEADS, lseq, kw), BF16),
                 jax.ShapeDtypeStruct((nb, ATT_KV_HEADS, lseq, 2 * kw), BF16)]
    if emit_kv:
        out_specs += [pl.BlockSpec((tm, kw), lambda i: (i, 0))] * 2
        out_shape += [jax.ShapeDtypeStruct((nrows, kw), F32)] * 2
    return pl.pallas_call(
        functools.partial(_qk_kernel, rope=tables is not None, emit_kv=emit_kv),
        grid=(nrows // tm,),
        in_specs=in_specs,
        out_specs=out_specs,
        out_shape=out_shape,
        compiler_params=_params("parallel"),
        name="qk_norm_rope" if tables is not None else "qk_norm",
    )(*args)


def _attn_kernel(*refs, cached):
    if cached:
        q_ref, kn_ref, vn_ref, kc_ref, vc_ref, o_ref = refs
    else:
        q_ref, kn_ref, vn_ref, o_ref = refs
    pair_w = 2 * HEAD_DIM
    tq = q_ref.shape[0]
    ck = min(ATT_KEY_CHUNK, kn_ref.shape[2])
    lo_half = lax.broadcasted_iota(jnp.int32, (1, pair_w), 1) < HEAD_DIM
    for pair in range(ATT_HEADS // 2):
        kv = pair // (ATT_HEADS // ATT_KV_HEADS // 2)
        qp = q_ref[:, pair * pair_w:(pair + 1) * pair_w]
        zero = jnp.zeros_like(qp)
        q2 = jnp.concatenate([jnp.where(lo_half, qp, zero), jnp.where(lo_half, zero, qp)], axis=0)
        chunks = [(kn_ref.at[0, kv], vn_ref.at[0, kv], j * ck, ck) for j in range(kn_ref.shape[2] // ck)]
        if cached:
            chunks.append((kc_ref.at[0, 0, kv], vc_ref.at[0, 0, kv], 0, kc_ref.shape[3]))
        parts = []
        for k_ref, v_ref, start, size in chunks:
            s = lax.dot_general(q2, k_ref[start:start + size, :], NT_DIMS, preferred_element_type=F32)
            m_c = jnp.max(s, axis=-1, keepdims=True)
            o_c = jnp.dot(jnp.exp2(s - m_c).astype(BF16), v_ref[start:start + size, :], preferred_element_type=F32)
            parts.append((m_c, o_c))
        m = functools.reduce(jnp.maximum, [m_c for m_c, _ in parts])
        o = sum(o_c * jnp.exp2(m_c - m) for m_c, o_c in parts)
        o = o[:, :pair_w] / o[:, pair_w:]
        o_ref[:, pair * pair_w:(pair + 1) * pair_w] = jnp.where(lo_half, o[:tq], o[tq:]).astype(o_ref.dtype)


def _attention(q, kz, vz, tq, cache=None):
    nb, nkv, l, kw = kz.shape
    per = l // tq
    in_specs = [
        pl.BlockSpec((tq, 512), lambda b, i: (b * per + i, 0)),
        pl.BlockSpec((1, nkv, l, kw), lambda b, i: (b, 0, 0, 0)),
        pl.BlockSpec((1, nkv, l, 2 * kw), lambda b, i: (b, 0, 0, 0)),
    ]
    args = [q, kz, vz]
    if cache is not None:
        kzc, vzc, layer = cache
        past = kzc.shape[3]
        in_specs += [pl.BlockSpec((1, 1, nkv, past, kw), lambda b, i: (b, layer, 0, 0, 0)),
                     pl.BlockSpec((1, 1, nkv, past, 2 * kw), lambda b, i: (b, layer, 0, 0, 0))]
        args += [kzc, vzc]
    return pl.pallas_call(
        functools.partial(_attn_kernel, cached=cache is not None),
        grid=(nb, per),
        in_specs=in_specs,
        out_specs=pl.BlockSpec((tq, 512), lambda b, i: (b * per + i, 0)),
        out_shape=jax.ShapeDtypeStruct((nb * l, 512), BF16),
        compiler_params=_params("parallel", "parallel"),
        name="attention_cached" if cache is not None else "attention",
    )(*args)


def _cache_variants(t, ones_tail):
    tb = t.astype(BF16).transpose(0, 1, 3, 2, 4)
    parts = [tb, tb] + ([jnp.ones(tb.shape[:-1] + (2 * HEAD_DIM,), BF16)] if ones_tail else [])
    return jnp.concatenate(parts, axis=-1)


def _glu(t):
    t = t.astype(F32)
    return t[:, :GROUP_W] * _sigmoid(t[:, GROUP_W:])


def _conf_kernel(m_ref, p_ref, n_ref, w_ref, b_ref, lg_ref, lb_ref, o_ref, scr, sh_scr, *, n_ctx_blocks, seq,
                 dec_seq):
    i = pl.program_id(0)
    tl = m_ref.shape[0]
    rows = tl + 2 * CONV_HALO
    lseq = jnp.where(i < n_ctx_blocks, seq, dec_seq)
    has_prev = ((i * tl) & (lseq - 1)) != 0
    has_next = (((i + 1) * tl) & (lseq - 1)) != 0
    scr[0:CONV_HALO, :] = jnp.where(has_prev, _glu(p_ref[...]), 0.0)
    scr[CONV_HALO:CONV_HALO + tl, :] = _glu(m_ref[...])
    scr[CONV_HALO + tl:rows, :] = jnp.where(has_next, _glu(n_ref[...]), 0.0)
    full = scr[...]
    for b in range(1, SUBLANES):
        sh_scr[b - 1] = pltpu.roll(full, rows - b, 0)
    acc = jnp.zeros((tl, GROUP_W), F32) + b_ref[...]
    off = CONV_HALO - CONV_WIDTH // 2
    for k in range(CONV_WIDTH):
        a, b = divmod(off + k, SUBLANES)
        src = scr if b == 0 else sh_scr.at[b - 1]
        acc = acc + src[a * SUBLANES:a * SUBLANES + tl, :] * w_ref[k:k + 1, :]
    mu = jnp.mean(acc, axis=-1, keepdims=True)
    xc = acc - mu
    var = jnp.mean(xc * xc, axis=-1, keepdims=True)
    y = xc * lax.rsqrt(var + EPS) * lg_ref[...] + lb_ref[...]
    o_ref[...] = (y * _sigmoid(y)).astype(o_ref.dtype)


def _conformer(u, w, b, lg, lb, n_ctx, seq, dec_seq):
    n = u.shape[0]
    tl = _pow2_tile(256, seq, dec_seq)
    hb = tl // CONV_HALO
    nhb = n // CONV_HALO
    gcol = COL_GLU // (2 * GROUP_W)
    return pl.pallas_call(
        functools.partial(_conf_kernel, n_ctx_blocks=n_ctx // tl, seq=seq, dec_seq=dec_seq),
        grid=(n // tl,),
        in_specs=[
            pl.BlockSpec((tl, 2 * GROUP_W), lambda i: (i, gcol)),
            pl.BlockSpec((CONV_HALO, 2 * GROUP_W), lambda i: (jnp.maximum(i * hb - 1, 0), gcol)),
            pl.BlockSpec((CONV_HALO, 2 * GROUP_W), lambda i: (jnp.minimum((i + 1) * hb, nhb - 1), gcol)),
            pl.BlockSpec((CONV_WIDTH, GROUP_W), lambda i: (0, 0)),
            pl.BlockSpec((1, GROUP_W), lambda i: (0, 0)),
            pl.BlockSpec((1, GROUP_W), lambda i: (0, 0)),
            pl.BlockSpec((1, GROUP_W), lambda i: (0, 0)),
        ],
        out_specs=pl.BlockSpec((tl, GROUP_W), lambda i: (i, 0)),
        out_shape=jax.ShapeDtypeStruct((n, GROUP_W), BF16),
        scratch_shapes=[pltpu.VMEM((tl + 2 * CONV_HALO, GROUP_W), F32),
                        pltpu.VMEM((SUBLANES - 1, tl + 2 * CONV_HALO, GROUP_W), F32)],
        compiler_params=_params("parallel"),
        name="conformer_conv",
    )(u, u, u, w, b.reshape(1, -1), lg.reshape(1, -1), lb.reshape(1, -1))


def _fnet_ch_kernel(x_ref, w_ref, o_ref):
    o_ref[...] = jnp.dot(x_ref[...], w_ref[...], preferred_element_type=F32).astype(o_ref.dtype)


def _fnet_channels(u, wch, tm):
    n = u.shape[0]
    return pl.pallas_call(
        _fnet_ch_kernel,
        grid=(n // tm,),
        in_specs=[
            pl.BlockSpec((tm, 512), lambda i: (i, COL_FOUR // 512)),
            pl.BlockSpec((512, 1024), lambda i: (0, 0)),
        ],
        out_specs=pl.BlockSpec((tm, 1024), lambda i: (i, 0)),
        out_shape=jax.ShapeDtypeStruct((n, 1024), BF16),
        compiler_params=_params("parallel"),
        name="fnet_channels",
    )(u, wch)


def _fnet_pos_kernel(wc_ref, ws_ref, z_ref, o_ref):
    o = jnp.dot(wc_ref[...], z_ref[:, :512], preferred_element_type=F32)
    o = o + jnp.dot(ws_ref[...], z_ref[:, 512:], preferred_element_type=F32)
    o_ref[...] = o.astype(o_ref.dtype)


def _fnet_positions(z, wc, ws, row0, nb, lseq):
    tm = min(512, lseq)
    r0 = row0 // lseq
    return pl.pallas_call(
        _fnet_pos_kernel,
        grid=(nb, lseq // tm),
        in_specs=[
            pl.BlockSpec((tm, lseq), lambda b, i: (i, 0)),
            pl.BlockSpec((tm, lseq), lambda b, i: (i, 0)),
            pl.BlockSpec((lseq, 1024), lambda b, i: (r0 + b, 0)),
        ],
        out_specs=pl.BlockSpec((tm, 512), lambda b, i: (b * (lseq // tm) + i, 0)),
        out_shape=jax.ShapeDtypeStruct((nb * lseq, 512), BF16),
        compiler_params=_params("parallel", "arbitrary"),
        name="fnet_positions",
    )(wc, ws, z)


def _dft_tables(n):
    rb = DFT_ROW_BLOCK if n % DFT_ROW_BLOCK == 0 else 1
    t = jnp.arange(n, dtype=jnp.int32)[None, :]

    def trig(rows):
        ang = ((rows[:, None] * t) % n).astype(F32) * (2.0 * np.pi / n)
        return jnp.cos(ang), jnp.sin(ang)

    ca, sa = trig(jnp.arange(n // rb, dtype=jnp.int32) * rb)
    cb, sb = trig(jnp.arange(rb, dtype=jnp.int32))
    scale = n ** -0.5
    cos = (ca[:, None, :] * cb[None, :, :] - sa[:, None, :] * sb[None, :, :]).reshape(n, n) * scale
    sin = (sa[:, None, :] * cb[None, :, :] + ca[:, None, :] * sb[None, :, :]).reshape(n, n) * scale
    return cos, sin


def _outproj_kernel(ys_ref, yc_ref, ac_ref, al_ref, fc_ref, fl_ref, w_ref, x_ref, mod_ref, o_ref, *, n_ctx_blocks):
    def body(att_ref, four_ref):
        acc = jnp.dot(ys_ref[...], w_ref[0, 0], preferred_element_type=F32)
        acc = acc + jnp.dot(att_ref[...], w_ref[0, 1], preferred_element_type=F32)
        acc = acc + jnp.dot(yc_ref[...], w_ref[0, 2], preferred_element_type=F32)
        acc = acc + jnp.dot(four_ref[...], w_ref[0, 3], preferred_element_type=F32)
        o_ref[...] = x_ref[...] + mod_ref[0, 0][2:3] * acc

    is_ctx = pl.program_id(0) < n_ctx_blocks
    pl.when(is_ctx)(lambda: body(ac_ref, fc_ref))
    pl.when(jnp.logical_not(is_ctx))(lambda: body(al_ref, fl_ref))


def _outproj(y_ssd, y_conv, att_c, att_l, four_c, four_l, w_all, x, mod_all, layer, mod_row, tm):
    n, d = x.shape
    tn = 1024
    nbc = att_c.shape[0] // tm
    nbl = att_l.shape[0] // tm
    full = pl.BlockSpec((tm, GROUP_W), lambda i, j: (i, 0))
    ctx = pl.BlockSpec((tm, GROUP_W), lambda i, j: (jnp.minimum(i, nbc - 1), 0))
    lat = pl.BlockSpec((tm, GROUP_W), lambda i, j: (jnp.clip(i - nbc, 0, nbl - 1), 0))
    return pl.pallas_call(
        functools.partial(_outproj_kernel, n_ctx_blocks=nbc),
        grid=(n // tm, d // tn),
        in_specs=[full, full, ctx, lat, ctx, lat,
                  pl.BlockSpec((1, 4, GROUP_W, tn), lambda i, j: (layer, 0, 0, j)),
                  pl.BlockSpec((tm, tn), lambda i, j: (i, j)),
                  pl.BlockSpec((1, 1, N_MOD, tn), lambda i, j: (layer, mod_row(i), 0, j))],
        out_specs=pl.BlockSpec((tm, tn), lambda i, j: (i, j)),
        out_shape=jax.ShapeDtypeStruct((n, d), F32),
        compiler_params=_params("parallel", "arbitrary"),
        name="outproj",
    )(y_ssd, y_conv, att_c, att_l, four_c, four_l, w_all, x, mod_all)


ROUTE_LANE0 = N_EXPERT_GROUPS


def _route(lg):
    lane = lax.broadcasted_iota(jnp.int32, lg.shape, 1)
    lanef = lane.astype(F32)
    ninf = -jnp.inf
    isg = lane < N_EXPERT_GROUPS
    mg = jnp.max(jnp.where(isg, lg, ninf), axis=-1, keepdims=True)
    gsum = jnp.sum(jnp.where(isg, jnp.exp(jnp.where(isg, lg, ninf) - mg), 0.0), axis=-1, keepdims=True)
    gwt = 1.0 / gsum
    gi = jnp.min(jnp.where(isg, jnp.where(lg == mg, lanef, 1e9), 1e9), axis=-1, keepdims=True)
    grp = jnp.where(lane < ROUTE_LANE0 + N_EXPERTS, (lane - ROUTE_LANE0) // EXPERTS_PER_GROUP, -1).astype(F32)
    el1 = jnp.where(grp == gi, lg, ninf)
    v1 = jnp.max(el1, axis=-1, keepdims=True)
    i1 = jnp.min(jnp.where(el1 == v1, lanef, 1e9), axis=-1, keepdims=True)
    el2 = jnp.where(lanef == i1, ninf, el1)
    v2 = jnp.max(el2, axis=-1, keepdims=True)
    i2 = jnp.min(jnp.where(el2 == v2, lanef, 1e9), axis=-1, keepdims=True)
    t = jnp.exp(v2 - v1)
    ew1 = 1.0 / (1.0 + t)
    ew2 = t * ew1
    return gwt * (jnp.where(lanef == i1, ew1, 0.0) + jnp.where(lanef == i2, ew2, 0.0))


def _moe_kernel(x_ref, g_ref, mod_ref, wr_ref, br_ref, w1_ref, w3_ref, w2_ref, o_ref, h_scr, comb_scr, *, ne):
    e = pl.program_id(1)

    @pl.when(e == 0)
    def _():
        x = x_ref[...]
        ms = jnp.mean(x * x, axis=-1, keepdims=True)
        y = x * lax.rsqrt(ms + EPS) * g_ref[...]
        m = mod_ref[0, 0]
        h = y * (1.0 + m[4:5]) + m[3:4]
        hi, lo = _split_bf16(h)
        h_scr[...] = hi
        lg = (jnp.dot(hi, wr_ref[0, 0], preferred_element_type=F32)
              + jnp.dot(lo, wr_ref[0, 0], preferred_element_type=F32)
              + jnp.dot(hi, wr_ref[0, 1], preferred_element_type=F32)) + br_ref[0]
        comb_scr[...] = _route(lg)
        o_ref[...] = jnp.zeros_like(o_ref)

    lane = lax.broadcasted_iota(jnp.int32, (1, LANES), 1)
    rows = MOE_ROW_CHUNK
    for r in range(0, h_scr.shape[0], rows):
        h = h_scr[r:r + rows, :]
        a = jnp.dot(h, w1_ref[0, 0], preferred_element_type=F32)
        b = jnp.dot(h, w3_ref[0, 0], preferred_element_type=F32)
        ce = jnp.sum(jnp.where(lane == e + ROUTE_LANE0, comb_scr[r:r + rows, :], 0.0), axis=-1, keepdims=True)
        hid = (a * _sigmoid(a)) * b * ce
        o_ref[r:r + rows, :] += jnp.dot(hid.astype(BF16), w2_ref[0, 0], preferred_element_type=F32)

    @pl.when(e == ne - 1)
    def _():
        o_ref[...] = x_ref[...] + mod_ref[0, 0][5:6] * o_ref[...]


def _moe(x, g, mod_all, wr, br, w1, w3, w2, layer, mod_row, tm):
    n, d = x.shape
    ne, ff = w1.shape[1], w1.shape[3]
    return pl.pallas_call(
        functools.partial(_moe_kernel, ne=ne),
        grid=(n // tm, ne),
        in_specs=[
            pl.BlockSpec((tm, d), lambda i, e: (i, 0)),
            pl.BlockSpec((1, d), lambda i, e: (0, 0)),
            pl.BlockSpec((1, 1, N_MOD, d), lambda i, e: (layer, mod_row(i), 0, 0)),
            pl.BlockSpec((1, 2, d, LANES), lambda i, e: (layer, 0, 0, 0)),
            pl.BlockSpec((1, 1, LANES), lambda i, e: (layer, 0, 0)),
            pl.BlockSpec((1, 1, d, ff), lambda i, e: (layer, e, 0, 0)),
            pl.BlockSpec((1, 1, d, ff), lambda i, e: (layer, e, 0, 0)),
            pl.BlockSpec((1, 1, ff, d), lambda i, e: (layer, e, 0, 0)),
        ],
        out_specs=pl.BlockSpec((tm, d), lambda i, e: (i, 0)),
        out_shape=jax.ShapeDtypeStruct((n, d), F32),
        scratch_shapes=[pltpu.VMEM((tm, d), BF16), pltpu.VMEM((tm, LANES), F32)],
        compiler_params=_params("parallel", "arbitrary", vmem=VMEM_LIMIT_MOE),
        name="moe",
    )(x, g.reshape(1, d), mod_all, wr, br, w1, w3, w2)


def _fnorm_kernel(x_ref, g_ref, o_ref):
    x = x_ref[...]
    ms = jnp.mean(x * x, axis=-1, keepdims=True)
    o_ref[...] = x * lax.rsqrt(ms + EPS) * g_ref[...]


def _final_norm(x, g, row0, nrows, tm):
    d = x.shape[1]
    r0 = row0 // tm
    return pl.pallas_call(
        _fnorm_kernel,
        grid=(nrows // tm,),
        in_specs=[pl.BlockSpec((tm, d), lambda i: (r0 + i, 0)), pl.BlockSpec((1, d), lambda i: (0, 0))],
        out_specs=pl.BlockSpec((tm, d), lambda i: (i, 0)),
        out_shape=jax.ShapeDtypeStruct((nrows, d), F32),
        compiler_params=_params("parallel"),
        name="final_norm",
    )(x, g.reshape(1, d))


def _rope_tables(dec_seq):
    t = jnp.arange(dec_seq, dtype=jnp.int32)
    half = ROPE_AXIS_DIM // 2
    freqs = ROPE_THETA ** (-jnp.arange(half, dtype=F32) / half)

    def axis(pos):
        ang = pos.astype(F32)[:, None] * freqs[None, :]
        c, s = jnp.cos(ang), jnp.sin(ang)
        return jnp.concatenate([c, c], axis=-1), jnp.concatenate([-s, s], axis=-1)

    cr, sr = axis(t // GRID_W)
    cc, sc = axis(t % GRID_W)
    cos = jnp.tile(jnp.concatenate([cr, cc], axis=-1), (1, ATT_HEADS))
    sin = jnp.tile(jnp.concatenate([sr, sc], axis=-1), (1, ATT_HEADS))
    return cos, sin


def _block_diag_ones(width, block):
    i = np.arange(width) // block
    return jnp.asarray((i[:, None] == i[None, :]).astype(np.float32), dtype=BF16)


def _pack_w_in(w_in):
    depth, d, _ = w_in.shape
    seg = lambda a, b: w_in[:, :, a:b]
    zeros = lambda n: jnp.zeros((depth, d, n), w_in.dtype)
    parts = [seg(SRC_XBC, SRC_DT), seg(SRC_GLU, SRC_FOUR), seg(SRC_Z, SRC_XBC), seg(SRC_Q, SRC_K),
             seg(SRC_FOUR, SRC_END), seg(SRC_K, SRC_V), seg(SRC_V, SRC_GLU),
             seg(SRC_DT, SRC_DT + SSD_HEADS), zeros(LANES - SSD_HEADS),
             seg(SRC_DT + SSD_HEADS, SRC_Q), zeros(LANES - SSD_HEADS)]
    return jnp.concatenate(parts, axis=-1).astype(BF16)


def _lanes8(v):
    return jnp.pad(v.astype(F32), ((0, 0), (0, 0), (0, LANES - SSD_HEADS)))[:, :, None, :]


def kernel(x_prompt, x_sample, cache_k, cache_v, state_ssd, c, c_ctx, norm1_g, norm2_g, w_mod, b_mod, w_in,
           ssd_conv_w, ssd_conv_b, ssd_dt_bias, ssd_A_log, ssd_D, ssd_norm_g, q_norm_g, k_norm_g, cf_dw_w, cf_dw_b,
           cf_ln_g, cf_ln_b, w_out, router_group_w, router_group_b, router_expert_w, router_expert_b, w1, w3, w2,
           final_norm_g):
    bc, lc, d = x_prompt.shape
    bl, ll, _ = x_sample.shape
    n_ctx, n_lat = bc * lc, bl * ll
    n = n_ctx + n_lat
    depth = w_in.shape[0]
    kvw = ATT_KV_HEADS * HEAD_DIM
    assert lc & (lc - 1) == 0 and ll & (ll - 1) == 0 and lc % SSD_CHUNK == 0 and ll % SSD_CHUNK == 0
    assert n_ctx % ll == 0 and ll % GRID_W == 0

    tm = _pow2_tile(1024, n_ctx, ll)
    tm_moe = tm
    tm_qk = _pow2_tile(512, lc, ll)

    def mod_row_fn(t):
        nb, per = n_ctx // t, ll // t
        return lambda i: jnp.where(i < nb, 0, 1 + (i - nb) // per)

    w_in_p = _pack_w_in(w_in)
    w_out_p = w_out.astype(BF16).reshape(depth, 4, GROUP_W, d)
    w1_b, w3_b, w2_b = w1.astype(BF16), w3.astype(BF16), w2.astype(BF16)
    wr = jnp.concatenate([router_group_w, router_expert_w,
                          jnp.zeros((depth, d, LANES - N_EXPERT_GROUPS - N_EXPERTS), F32)], axis=-1)
    wr_hi = wr.astype(BF16)
    wr_lo = (wr - wr_hi.astype(F32)).astype(BF16)
    wr_p = jnp.stack([wr_hi, wr_lo], axis=1)
    br_p = jnp.concatenate([router_group_b, router_expert_b,
                            jnp.zeros((depth, LANES - N_EXPERT_GROUPS - N_EXPERTS), F32)], axis=-1)[:, None, :]
    dt_bias_p = _lanes8(ssd_dt_bias)
    a_log_p = _lanes8(ssd_A_log)
    d_lanes = jnp.repeat(ssd_D.astype(F32), SSD_HEADDIM, axis=-1)[:, None, :]
    expand_np = np.arange(LANES)[:, None] == (np.arange(512)[None, :] // SSD_HEADDIM)
    expand = jnp.asarray(expand_np, dtype=BF16)
    expand_t = jnp.asarray(expand_np.T, dtype=BF16)
    gq = jnp.tile(q_norm_g, (1, ATT_HEADS))[:, None, :]
    gk = jnp.tile(k_norm_g, (1, ATT_KV_HEADS))[:, None, :]
    bdq = _block_diag_ones(512, HEAD_DIM)
    bdk = _block_diag_ones(kvw, HEAD_DIM)
    rope_tabs = _rope_tables(ll)
    cch, sch = _dft_tables(FNET_GROUP_CH)
    eye4 = jnp.eye(4, dtype=F32)
    wch = jnp.concatenate([jnp.kron(eye4, cch), jnp.kron(eye4, sch)], axis=-1).astype(BF16)
    dft_c = {}
    for lseq in (lc, ll):
        cl, sl = _dft_tables(lseq)
        dft_c[lseq] = (cl.astype(BF16), (-sl).astype(BF16))
    kz_cache = _cache_variants(cache_k, ones_tail=False)
    vz_cache = _cache_variants(cache_v, ones_tail=True)

    mod_all = _adaln(jnp.concatenate([c_ctx[None, :], c], axis=0), w_mod, b_mod)

    x = jnp.concatenate([x_prompt.reshape(n_ctx, d), x_sample.reshape(n_lat, d)], axis=0)
    h0_ctx = jnp.zeros((bc, 2, 512, SSD_D_STATE), F32)
    h0_lat = state_ssd.reshape(bl, depth, 2, 512, SSD_D_STATE)
    ks, vs, ss = [], [], []
    for l in range(depth):
        u = _inproj(x, norm1_g[l], mod_all, w_in_p, l, mod_row_fn(tm), tm)

        xbc = _ssdpre(u, ssd_conv_w[l], ssd_conv_b[l], n_ctx, lc, ll)
        yf_c, yb_c, h_c = _ssd_scan(xbc, u, dt_bias_p[l], a_log_p[l], expand, expand_t, h0_ctx, 0, bc, lc)
        yf_l, yb_l, _ = _ssd_scan(xbc, u, dt_bias_p[l], a_log_p[l], expand, expand_t, h0_lat[:, l], n_ctx, bl, ll)
        y_ssd = _gnorm((yf_c, yb_c), (yf_l, yb_l), xbc, u, d_lanes[l], ssd_norm_g[l], tm)
        ss.append(h_c.reshape(bc, 2, SSD_HEADS, SSD_HEADDIM, SSD_D_STATE))

        q_c, kz_c, vz_c, k_c, v_c = _qk_prep(u, gq[l], gk[l], bdq, bdk, 0, bc, lc, tm_qk, emit_kv=True)
        q_l, kz_l, vz_l = _qk_prep(u, gq[l], gk[l], bdq, bdk, n_ctx, bl, ll, tm_qk, tables=rope_tabs)
        ks.append(k_c.reshape(bc, lc, ATT_KV_HEADS, HEAD_DIM))
        vs.append(v_c.reshape(bc, lc, ATT_KV_HEADS, HEAD_DIM))
        att_c = _attention(q_c, kz_c, vz_c, min(256, lc))
        att_l = _attention(q_l, kz_l, vz_l, min(256, ll), cache=(kz_cache, vz_cache, l))

        y_conv = _conformer(u, cf_dw_w[l], cf_dw_b[l], cf_ln_g[l], cf_ln_b[l], n_ctx, lc, ll)

        zf = _fnet_channels(u, wch, tm)
        four_c = _fnet_positions(zf, *dft_c[lc], 0, bc, lc)
        four_l = _fnet_positions(zf, *dft_c[ll], n_ctx, bl, ll)

        x = _outproj(y_ssd, y_conv, att_c, att_l, four_c, four_l, w_out_p, x, mod_all, l, mod_row_fn(tm), tm)
        x = _moe(x, norm2_g[l], mod_all, wr_p, br_p, w1_b, w3_b, w2_b, l, mod_row_fn(tm_moe), tm_moe)

    y_prompt = _final_norm(x, final_norm_g, 0, n_ctx, tm).reshape(bc, lc, d)
    y_sample = _final_norm(x, final_norm_g, n_ctx, n_lat, tm).reshape(bl, ll, d)
    return (y_prompt, y_sample, jnp.stack(ks, axis=1), jnp.stack(vs, axis=1), jnp.stack(ss, axis=1))
```

```python
import functools

import numpy as np
import jax
import jax.numpy as jnp
from jax import lax
from jax.experimental import pallas as pl
from jax.experimental.pallas import tpu as pltpu

F32 = jnp.float32
BF16 = jnp.bfloat16
HIGHEST = lax.Precision.HIGHEST

D_MODEL = 2048
DEPTH = 4
GRID_W = 64
GROUP_W = 512
SSD_HEADS = 8
SSD_HEADDIM = 64
SSD_GROUPS = 2
SSD_D_STATE = 128
SSD_CHUNK = 128
HEAD_DIM = 64
ATT_HEADS = 8
ATT_KV_HEADS = 2
ROPE_THETA = 10000.0
ROPE_AXIS_DIM = 32
CONV_WIDTH = 31
CONV_HALO = 16
SUBLANES = 8
FNET_GROUP_CH = 128
DFT_ROW_BLOCK = 64
N_EXPERT_GROUPS = 4
EXPERTS_PER_GROUP = 4
N_EXPERTS = 16
EXPERT_FF = 256
N_MOD = 6
EPS = 1e-6
LANES = 128

U_WIDTH = 4096
COL_XBC, COL_GLU, COL_Z, COL_Q, COL_FOUR, COL_K, COL_V, COL_DTF, COL_DTB = (
    0, 1024, 2048, 2560, 3072, 3584, 3712, 3840, 3968)
SRC_Z, SRC_XBC, SRC_DT, SRC_Q, SRC_K, SRC_V, SRC_GLU, SRC_FOUR, SRC_END = (
    0, 512, 1536, 1552, 2064, 2192, 2320, 3344, 3856)

VMEM_LIMIT = 48 * 1024 * 1024
VMEM_LIMIT_MOE = 54 * 1024 * 1024

NT_DIMS = (((1,), (1,)), ((), ()))
Q_SCALE = HEAD_DIM ** -0.5 * float(np.log2(np.e))
ATT_KEY_CHUNK = 1024
MOE_BLOCK = 128
MOE_EXPERTS_PER_STEP = 2
assert EXPERTS_PER_GROUP % MOE_EXPERTS_PER_STEP == 0
SSD_SEQ_BLOCK = 2


def _params(*sem, vmem=VMEM_LIMIT):
    return pltpu.CompilerParams(dimension_semantics=sem, vmem_limit_bytes=vmem)


def _sigmoid(x):
    return 1.0 / (1.0 + jnp.exp(-x))


def _split_bf16(x):
    hi = x.astype(BF16)
    lo = (x - hi.astype(F32)).astype(BF16)
    return hi, lo


def _pow2_tile(pref, *dims):
    t = pref
    while any(d % t for d in dims):
        t //= 2
    return t


def _mod_kernel(c_ref, w_ref, b_ref, o_ref, acc_ref, *, nk):
    k = pl.program_id(2)

    @pl.when(k == 0)
    def _():
        acc_ref[...] = jnp.zeros_like(acc_ref)

    w = w_ref[0]
    tk, tn = w.shape
    for r in range(c_ref.shape[0]):
        c = c_ref[r]
        s = c * _sigmoid(c)
        acc_ref[r] += (s * w).reshape(tk // SUBLANES, SUBLANES, tn).sum(axis=0)

    @pl.when(k == nk - 1)
    def _():
        o_ref[0] = acc_ref[...].sum(axis=1) + b_ref[0]


def _adaln(cvec, w_mod, b_mod):
    r, d = cvec.shape
    depth, _, n = w_mod.shape
    tk, tn = 256, 2048
    nk = d // tk
    out = pl.pallas_call(
        functools.partial(_mod_kernel, nk=nk),
        grid=(depth, n // tn, nk),
        in_specs=[
            pl.BlockSpec((r, tk, 1), lambda l, j, k: (0, k, 0)),
            pl.BlockSpec((1, tk, tn), lambda l, j, k: (l, k, j)),
            pl.BlockSpec((1, 1, tn), lambda l, j, k: (l, 0, j)),
        ],
        out_specs=pl.BlockSpec((1, r, tn), lambda l, j, k: (l, 0, j)),
        out_shape=jax.ShapeDtypeStruct((depth, r, n), F32),
        scratch_shapes=[pltpu.VMEM((r, SUBLANES, tn), F32)],
        compiler_params=_params("parallel", "parallel", "arbitrary"),
        name="adaln",
    )(cvec.reshape(r, d, 1), w_mod, b_mod.reshape(depth, 1, n))
    return out.reshape(depth, r, N_MOD, d)


def _inproj_kernel(x_ref, g_ref, mod_ref, w_ref, o_ref):
    x = x_ref[...]
    ms = jnp.mean(x * x, axis=-1, keepdims=True)
    y = x * lax.rsqrt(ms + EPS) * g_ref[...]
    m = mod_ref[0, 0]
    h = (y * (1.0 + m[1:2]) + m[0:1]).astype(BF16)
    o_ref[...] = jnp.dot(h, w_ref[0], preferred_element_type=F32).astype(o_ref.dtype)


def _inproj(x, g, mod_all, w_all, layer, mod_row, tm):
    n, d = x.shape
    nout = w_all.shape[2]
    tn = 2048
    return pl.pallas_call(
        _inproj_kernel,
        grid=(n // tm, nout // tn),
        in_specs=[
            pl.BlockSpec((tm, d), lambda i, j: (i, 0)),
            pl.BlockSpec((1, d), lambda i, j: (0, 0)),
            pl.BlockSpec((1, 1, N_MOD, d), lambda i, j: (layer, mod_row(i), 0, 0)),
            pl.BlockSpec((1, d, tn), lambda i, j: (layer, 0, j)),
        ],
        out_specs=pl.BlockSpec((tm, tn), lambda i, j: (i, j)),
        out_shape=jax.ShapeDtypeStruct((n, nout), BF16),
        compiler_params=_params("parallel", "arbitrary"),
        name="inproj",
    )(x, g.reshape(1, d), mod_all, w_all)


def _ssdpre_kernel(x_ref, w_ref, b_ref, o_ref, *, n_ctx_blocks, seq, dec_seq):
    x = x_ref[...].astype(F32)
    tl = x.shape[0]
    lseq = jnp.where(pl.program_id(0) < n_ctx_blocks, seq, dec_seq)
    pos = lax.broadcasted_iota(jnp.int32, (tl, 1), 0) & (lseq - 1)
    xp = jnp.where(pos == 0, 0.0, pltpu.roll(x, 1, 0))
    xn = jnp.where(pos == lseq - 1, 0.0, pltpu.roll(x, tl - 1, 0))
    w = w_ref[...]
    y = xp * w[0:1] + x * w[1:2] + xn * w[2:3] + b_ref[...]
    o_ref[...] = (y * _sigmoid(y)).astype(o_ref.dtype)


def _ssdpre(u, w, b, n_ctx, seq, dec_seq):
    n = u.shape[0]
    c = w.shape[1]
    tl = max(seq, dec_seq)
    assert n_ctx % tl == 0 and tl % seq == 0 and tl % dec_seq == 0
    tc = 256
    return pl.pallas_call(
        functools.partial(_ssdpre_kernel, n_ctx_blocks=n_ctx // tl, seq=seq, dec_seq=dec_seq),
        grid=(n // tl, c // tc),
        in_specs=[
            pl.BlockSpec((tl, tc), lambda i, j: (i, j + COL_XBC // tc)),
            pl.BlockSpec((3, tc), lambda i, j: (0, j)),
            pl.BlockSpec((1, tc), lambda i, j: (0, j)),
        ],
        out_specs=pl.BlockSpec((tl, tc), lambda i, j: (i, j)),
        out_shape=jax.ShapeDtypeStruct((n, c), BF16),
        compiler_params=_params("parallel", "parallel"),
        name="ssd_conv",
    )(u, w, b.reshape(1, c))


def _ssd_direction(d, x_ref, bc_ref, dt_ref, bias, alog, e, et, h_scr, y_ref):
    q = x_ref.shape[0]
    hpg = SSD_HEADS // SSD_GROUPS
    gw = hpg * SSD_HEADDIM
    x = x_ref[...].astype(F32)
    bc = bc_ref[...]
    raw = dt_ref[...].astype(F32) + bias
    dt = jnp.maximum(raw, 0.0) + jnp.log1p(jnp.exp(-jnp.abs(raw)))
    a = dt * (-jnp.exp(alog))
    ri = lax.broadcasted_iota(jnp.int32, (q, q), 0)
    ci = lax.broadcasted_iota(jnp.int32, (q, q), 1)
    tri = (ri >= ci).astype(F32)
    cs = jnp.dot(tri, a, precision=HIGHEST, preferred_element_type=F32)
    total = cs[q - 1:q, :]
    csa = cs if d == 0 else total - cs + a
    mask = (ri >= ci) if d == 0 else (ri <= ci)
    lhs = jnp.concatenate([dt * jnp.exp(total - csa), jnp.exp(csa)], axis=0)
    hi, lo = _split_bf16(lhs)
    ex = jnp.dot(hi, e, preferred_element_type=F32) + jnp.dot(lo, e, preferred_element_type=F32)
    w_x = ex[0:q]
    ecs_x = ex[q:2 * q]
    cd_t = jnp.exp(jnp.broadcast_to(total, (LANES, LANES)).T)
    hi, lo = _split_bf16(cd_t)
    cd_rows = jnp.dot(et, hi, preferred_element_type=F32) + jnp.dot(et, lo, preferred_element_type=F32)
    csa_t = csa.T
    dt_t = dt.T
    xb = x_ref[...]
    xw = x * w_x
    lane_head = lax.broadcasted_iota(jnp.int32, (1, gw), 1) // SSD_HEADDIM
    for g in range(SSD_GROUPS):
        bm = bc[:, g * SSD_D_STATE:(g + 1) * SSD_D_STATE]
        cm = bc[:, (SSD_GROUPS + g) * SSD_D_STATE:(SSD_GROUPS + g + 1) * SSD_D_STATE]
        cb = lax.dot_general(cm, bm, NT_DIMS, preferred_element_type=F32)
        xg = xb[:, g * gw:(g + 1) * gw]
        h_g = h_scr[g * gw:(g + 1) * gw, :]
        y_g = lax.dot_general(cm, h_g.astype(BF16), NT_DIMS, preferred_element_type=F32)
        y_g = y_g * ecs_x[:, g * gw:(g + 1) * gw]
        for hh in range(hpg):
            h = g * hpg + hh
            seg = csa[:, h:h + 1] - csa_t[h:h + 1, :]
            decay = jnp.exp(jnp.where(mask, seg, -jnp.inf))
            sc = (cb * decay * dt_t[h:h + 1, :]).astype(BF16)
            xm = jnp.where(lane_head == hh, xg, jnp.zeros_like(xg))
            y_g = y_g + jnp.dot(sc, xm, preferred_element_type=F32)
        xw_t = xw[:, g * gw:(g + 1) * gw].T.astype(BF16)
        st = jnp.dot(xw_t, bm, preferred_element_type=F32)
        y_ref[:, g * gw:(g + 1) * gw] = y_g.astype(y_ref.dtype)
        h_scr[g * gw:(g + 1) * gw, :] = cd_rows[g * gw:(g + 1) * gw, :] * h_g + st


def _ssd_kernel(xf_ref, bcf_ref, dtf_ref, xb_ref, bcb_ref, dtb_ref, bias_ref, alog_ref, e_ref, et_ref, h0_ref,
                yf_ref, yb_ref, hout_ref, h_scr, *, nc):
    c = pl.program_id(1)

    @pl.when(c == 0)
    def _():
        h_scr[...] = h0_ref[...]

    e = e_ref[...]
    et = et_ref[...]
    for s in range(xf_ref.shape[0]):
        _ssd_direction(0, xf_ref.at[s], bcf_ref.at[s], dtf_ref.at[s], bias_ref[0], alog_ref[0], e, et,
                       h_scr.at[s, 0], yf_ref.at[s])
        _ssd_direction(1, xb_ref.at[s], bcb_ref.at[s], dtb_ref.at[s], bias_ref[1], alog_ref[1], e, et,
                       h_scr.at[s, 1], yb_ref.at[s])

    @pl.when(c == nc - 1)
    def _():
        hout_ref[...] = h_scr[...]


def _ssd_scan(xbc, u, bias, alog, expand, expand_t, h0, row0, nb, lseq):
    q = SSD_CHUNK
    nc = lseq // q
    n = xbc.shape[0]
    s0 = row0 // lseq
    sb = SSD_SEQ_BLOCK if (nb % SSD_SEQ_BLOCK == 0 and s0 % SSD_SEQ_BLOCK == 0) else 1
    xbc3 = xbc.reshape(n // lseq, lseq, xbc.shape[1])
    u3 = u.reshape(n // lseq, lseq, u.shape[1])
    sq = lambda b: s0 // sb + b
    sspec = pl.BlockSpec((sb, 2, 512, SSD_D_STATE), lambda b, c: (b, 0, 0, 0))
    yf, yb, hout = pl.pallas_call(
        functools.partial(_ssd_kernel, nc=nc),
        grid=(nb // sb, nc),
        in_specs=[
            pl.BlockSpec((sb, q, 512), lambda b, c: (sq(b), c, 0)),
            pl.BlockSpec((sb, q, 512), lambda b, c: (sq(b), c, 1)),
            pl.BlockSpec((sb, q, LANES), lambda b, c: (sq(b), c, COL_DTF // LANES)),
            pl.BlockSpec((sb, q, 512), lambda b, c: (sq(b), nc - 1 - c, 0)),
            pl.BlockSpec((sb, q, 512), lambda b, c: (sq(b), nc - 1 - c, 1)),
            pl.BlockSpec((sb, q, LANES), lambda b, c: (sq(b), nc - 1 - c, COL_DTB // LANES)),
            pl.BlockSpec((2, 1, LANES), lambda b, c: (0, 0, 0)),
            pl.BlockSpec((2, 1, LANES), lambda b, c: (0, 0, 0)),
            pl.BlockSpec((LANES, 512), lambda b, c: (0, 0)),
            pl.BlockSpec((512, LANES), lambda b, c: (0, 0)),
            sspec,
        ],
        out_specs=[
            pl.BlockSpec((sb, q, 512), lambda b, c: (b, c, 0)),
            pl.BlockSpec((sb, q, 512), lambda b, c: (b, nc - 1 - c, 0)),
            sspec,
        ],
        out_shape=[
            jax.ShapeDtypeStruct((nb, lseq, 512), BF16),
            jax.ShapeDtypeStruct((nb, lseq, 512), BF16),
            jax.ShapeDtypeStruct((nb, 2, 512, SSD_D_STATE), F32),
        ],
        scratch_shapes=[pltpu.VMEM((sb, 2, 512, SSD_D_STATE), F32)],
        compiler_params=_params("parallel", "arbitrary"),
        name="ssd_scan",
    )(xbc3, xbc3, u3, xbc3, xbc3, u3, bias, alog, expand, expand_t, h0)
    return yf.reshape(nb * lseq, 512), yb.reshape(nb * lseq, 512), hout


def _gnorm_kernel(yfc_ref, ybc_ref, yfl_ref, ybl_ref, x_ref, z_ref, dl_ref, g_ref, o_ref, *, n_ctx_blocks):
    def body(yf_ref, yb_ref):
        y = yf_ref[...].astype(F32) + yb_ref[...].astype(F32) + x_ref[...].astype(F32) * dl_ref[...]
        z = z_ref[...].astype(F32)
        t = y * (z * _sigmoid(z))
        ms = jnp.mean(t * t, axis=-1, keepdims=True)
        o_ref[...] = (t * lax.rsqrt(ms + EPS) * g_ref[...]).astype(o_ref.dtype)

    is_ctx = pl.program_id(0) < n_ctx_blocks
    pl.when(is_ctx)(lambda: body(yfc_ref, ybc_ref))
    pl.when(jnp.logical_not(is_ctx))(lambda: body(yfl_ref, ybl_ref))


def _gnorm(y_c, y_l, xbc, u, d_lanes, g, tm):
    n = xbc.shape[0]
    nbc = y_c[0].shape[0] // tm
    nbl = y_l[0].shape[0] // tm
    ctx = pl.BlockSpec((tm, 512), lambda i: (jnp.minimum(i, nbc - 1), 0))
    lat = pl.BlockSpec((tm, 512), lambda i: (jnp.clip(i - nbc, 0, nbl - 1), 0))
    return pl.pallas_call(
        functools.partial(_gnorm_kernel, n_ctx_blocks=nbc),
        grid=(n // tm,),
        in_specs=[
            ctx, ctx, lat, lat,
            pl.BlockSpec((tm, 512), lambda i: (i, 0)),
            pl.BlockSpec((tm, 512), lambda i: (i, COL_Z // 512)),
            pl.BlockSpec((1, 512), lambda i: (0, 0)),
            pl.BlockSpec((1, 512), lambda i: (0, 0)),
        ],
        out_specs=pl.BlockSpec((tm, 512), lambda i: (i, 0)),
        out_shape=jax.ShapeDtypeStruct((n, 512), BF16),
        compiler_params=_params("parallel"),
        name="ssd_gate_norm",
    )(*y_c, *y_l, xbc, u, d_lanes, g.reshape(1, 512))


def _head_rms(t, bd, gain):
    hi, lo = _split_bf16(t * t)
    ss = jnp.dot(hi, bd, preferred_element_type=F32) + jnp.dot(lo, bd, preferred_element_type=F32)
    return t * lax.rsqrt(ss * (1.0 / HEAD_DIM) + EPS) * gain


def _rope(t, cos, sin):
    w = t.shape[1]
    lane = lax.broadcasted_iota(jnp.int32, (1, w), 1)
    first = (lane & (ROPE_AXIS_DIM - 1)) < ROPE_AXIS_DIM // 2
    partner = jnp.where(first, pltpu.roll(t, w - ROPE_AXIS_DIM // 2, 1), pltpu.roll(t, ROPE_AXIS_DIM // 2, 1))
    return t * cos + partner * sin


def _dup_variants(t, o_ref, ones_tail):
    w = 2 * HEAD_DIM
    lane = lax.broadcasted_iota(jnp.int32, (1, w), 1)
    lo_half = lane < HEAD_DIM
    sw = pltpu.roll(t, HEAD_DIM, 1)
    o_ref[0, 0, :, 0:w] = jnp.where(lo_half, t, sw).astype(o_ref.dtype)
    o_ref[0, 1, :, 0:w] = jnp.where(lo_half, sw, t).astype(o_ref.dtype)
    if ones_tail:
        ones = jnp.ones(t.shape, o_ref.dtype)
        o_ref[0, 0, :, w:2 * w] = ones
        o_ref[0, 1, :, w:2 * w] = ones


def _qk_kernel(*refs, rope, emit_kv):
    q_ref, k_ref, v_ref, gq_ref, gk_ref, bdq_ref, bdk_ref = refs[:7]
    refs = refs[7:]
    if rope:
        cos_ref, sin_ref, cosk_ref, sink_ref = refs[:4]
        refs = refs[4:]
    qo_ref, kz_ref, vz_ref = refs[:3]
    qn = _head_rms(q_ref[...].astype(F32), bdq_ref[...], gq_ref[...])
    kn = _head_rms(k_ref[...].astype(F32), bdk_ref[...], gk_ref[...])
    v = v_ref[...].astype(F32)
    if emit_kv:
        ko_ref, vo_ref = refs[3:5]
        ko_ref[...] = kn
        vo_ref[...] = v
    if rope:
        qn = _rope(qn, cos_ref[...], sin_ref[...])
        kn = _rope(kn, cosk_ref[...], sink_ref[...])
    qo_ref[...] = (qn * Q_SCALE).astype(qo_ref.dtype)
    _dup_variants(kn, kz_ref, ones_tail=False)
    _dup_variants(v, vz_ref, ones_tail=True)


def _qk_prep(u, gq, gk, bdq, bdk, row0, nb, lseq, tm, tables=None, emit_kv=False):
    kw = ATT_KV_HEADS * HEAD_DIM
    nrows = nb * lseq
    r0 = row0 // tm
    per = lseq // tm
    in_specs = [
        pl.BlockSpec((tm, 512), lambda i: (r0 + i, COL_Q // 512)),
        pl.BlockSpec((tm, kw), lambda i: (r0 + i, COL_K // kw)),
        pl.BlockSpec((tm, kw), lambda i: (r0 + i, COL_V // kw)),
        pl.BlockSpec((1, 512), lambda i: (0, 0)),
        pl.BlockSpec((1, kw), lambda i: (0, 0)),
        pl.BlockSpec((512, 512), lambda i: (0, 0)),
        pl.BlockSpec((kw, kw), lambda i: (0, 0)),
    ]
    args = [u, u, u, gq, gk, bdq, bdk]
    if tables is not None:
        cos, sin = tables
        in_specs += [
            pl.BlockSpec((tm, 512), lambda i: (i % per, 0)),
            pl.BlockSpec((tm, 512), lambda i: (i % per, 0)),
            pl.BlockSpec((tm, kw), lambda i: (i % per, 0)),
            pl.BlockSpec((tm, kw), lambda i: (i % per, 0)),
        ]
        args += [cos, sin, cos, sin]
    zspec = pl.BlockSpec((1, ATT_KV_HEADS, tm, kw), lambda i: (i // per, 0, i % per, 0))
    out_specs = [pl.BlockSpec((tm, 512), lambda i: (i, 0)), zspec, zspec]
    vspec = pl.BlockSpec((1, ATT_KV_HEADS, tm, 2 * kw), lambda i: (i // per, 0, i % per, 0))
    out_specs[2] = vspec
    out_shape = [jax.ShapeDtypeStruct((nrows, 512), BF16),
                 jax.ShapeDtypeStruct((nb, ATT_KV_HEADS, lseq, kw), BF16),
                 jax.ShapeDtypeStruct((nb, ATT_KV_HEADS, lseq, 2 * kw), BF16)]
    if emit_kv:
        out_specs += [pl.BlockSpec((tm, kw), lambda i: (i, 0))] * 2
        out_shape += [jax.ShapeDtypeStruct((nrows, kw), F32)] * 2
    return pl.pallas_call(
        functools.partial(_qk_kernel, rope=tables is not None, emit_kv=emit_kv),
        grid=(nrows // tm,),
        in_specs=in_specs,
        out_specs=out_specs,
        out_shape=out_shape,
        compiler_params=_params("parallel"),
        name="qk_norm_rope" if tables is not None else "qk_norm",
    )(*args)


def _attn_kernel(*refs, cached):
    if cached:
        q_ref, kn_ref, vn_ref, kc_ref, vc_ref, o_ref = refs
    else:
        q_ref, kn_ref, vn_ref, o_ref = refs
    pair_w = 2 * HEAD_DIM
    tq = q_ref.shape[0]
    ck = min(ATT_KEY_CHUNK, kn_ref.shape[2])
    lo_half = lax.broadcasted_iota(jnp.int32, (1, pair_w), 1) < HEAD_DIM
    for pair in range(ATT_HEADS // 2):
        kv = pair // (ATT_HEADS // ATT_KV_HEADS // 2)
        qp = q_ref[:, pair * pair_w:(pair + 1) * pair_w]
        zero = jnp.zeros_like(qp)
        q2 = jnp.concatenate([jnp.where(lo_half, qp, zero), jnp.where(lo_half, zero, qp)], axis=0)
        chunks = [(kn_ref.at[0, kv], vn_ref.at[0, kv], j * ck, ck) for j in range(kn_ref.shape[2] // ck)]
        if cached:
            chunks.append((kc_ref.at[0, 0, kv], vc_ref.at[0, 0, kv], 0, kc_ref.shape[3]))
        parts = []
        for k_ref, v_ref, start, size in chunks:
            s = lax.dot_general(q2, k_ref[start:start + size, :], NT_DIMS, preferred_element_type=F32)
            m_c = jnp.max(s, axis=-1, keepdims=True)
            o_c = jnp.dot(jnp.exp2(s - m_c).astype(BF16), v_ref[start:start + size, :], preferred_element_type=F32)
            parts.append((m_c, o_c))
        m = functools.reduce(jnp.maximum, [m_c for m_c, _ in parts])
        o = sum(o_c * jnp.exp2(m_c - m) for m_c, o_c in parts)
        o = o[:, :pair_w] / o[:, pair_w:]
        o_ref[:, pair * pair_w:(pair + 1) * pair_w] = jnp.where(lo_half, o[:tq], o[tq:]).astype(o_ref.dtype)


def _attention(q, kz, vz, tq, cache=None):
    nb, nkv, l, kw = kz.shape
    per = l // tq
    in_specs = [
        pl.BlockSpec((tq, 512), lambda b, i: (b * per + i, 0)),
        pl.BlockSpec((1, nkv, l, kw), lambda b, i: (b, 0, 0, 0)),
        pl.BlockSpec((1, nkv, l, 2 * kw), lambda b, i: (b, 0, 0, 0)),
    ]
    args = [q, kz, vz]
    if cache is not None:
        kzc, vzc, layer = cache
        past = kzc.shape[3]
        in_specs += [pl.BlockSpec((1, 1, nkv, past, kw), lambda b, i: (b, layer, 0, 0, 0)),
                     pl.BlockSpec((1, 1, nkv, past, 2 * kw), lambda b, i: (b, layer, 0, 0, 0))]
        args += [kzc, vzc]
    return pl.pallas_call(
        functools.partial(_attn_kernel, cached=cache is not None),
        grid=(nb, per),
        in_specs=in_specs,
        out_specs=pl.BlockSpec((tq, 512), lambda b, i: (b * per + i, 0)),
        out_shape=jax.ShapeDtypeStruct((nb * l, 512), BF16),
        compiler_params=_params("parallel", "parallel"),
        name="attention_cached" if cache is not None else "attention",
    )(*args)


def _cache_variants(t, ones_tail):
    tb = t.astype(BF16).transpose(0, 1, 3, 2, 4)
    parts = [tb, tb] + ([jnp.ones(tb.shape[:-1] + (2 * HEAD_DIM,), BF16)] if ones_tail else [])
    return jnp.concatenate(parts, axis=-1)


def _glu(t):
    t = t.astype(F32)
    return t[:, :GROUP_W] * _sigmoid(t[:, GROUP_W:])


def _conf_kernel(m_ref, p_ref, n_ref, w_ref, b_ref, lg_ref, lb_ref, o_ref, scr, sh_scr, *, n_ctx_blocks, seq,
                 dec_seq):
    i = pl.program_id(0)
    tl = m_ref.shape[0]
    rows = tl + 2 * CONV_HALO
    lseq = jnp.where(i < n_ctx_blocks, seq, dec_seq)
    has_prev = ((i * tl) & (lseq - 1)) != 0
    has_next = (((i + 1) * tl) & (lseq - 1)) != 0
    scr[0:CONV_HALO, :] = jnp.where(has_prev, _glu(p_ref[...]), 0.0)
    scr[CONV_HALO:CONV_HALO + tl, :] = _glu(m_ref[...])
    scr[CONV_HALO + tl:rows, :] = jnp.where(has_next, _glu(n_ref[...]), 0.0)
    full = scr[...]
    for b in range(1, SUBLANES):
        sh_scr[b - 1] = pltpu.roll(full, rows - b, 0)
    acc = jnp.zeros((tl, GROUP_W), F32) + b_ref[...]
    off = CONV_HALO - CONV_WIDTH // 2
    for k in range(CONV_WIDTH):
        a, b = divmod(off + k, SUBLANES)
        src = scr if b == 0 else sh_scr.at[b - 1]
        acc = acc + src[a * SUBLANES:a * SUBLANES + tl, :] * w_ref[k:k + 1, :]
    mu = jnp.mean(acc, axis=-1, keepdims=True)
    xc = acc - mu
    var = jnp.mean(xc * xc, axis=-1, keepdims=True)
    y = xc * lax.rsqrt(var + EPS) * lg_ref[...] + lb_ref[...]
    o_ref[...] = (y * _sigmoid(y)).astype(o_ref.dtype)


def _conformer(u, w, b, lg, lb, n_ctx, seq, dec_seq):
    n = u.shape[0]
    tl = _pow2_tile(256, seq, dec_seq)
    hb = tl // CONV_HALO
    nhb = n // CONV_HALO
    gcol = COL_GLU // (2 * GROUP_W)
    return pl.pallas_call(
        functools.partial(_conf_kernel, n_ctx_blocks=n_ctx // tl, seq=seq, dec_seq=dec_seq),
        grid=(n // tl,),
        in_specs=[
            pl.BlockSpec((tl, 2 * GROUP_W), lambda i: (i, gcol)),
            pl.BlockSpec((CONV_HALO, 2 * GROUP_W), lambda i: (jnp.maximum(i * hb - 1, 0), gcol)),
            pl.BlockSpec((CONV_HALO, 2 * GROUP_W), lambda i: (jnp.minimum((i + 1) * hb, nhb - 1), gcol)),
            pl.BlockSpec((CONV_WIDTH, GROUP_W), lambda i: (0, 0)),
            pl.BlockSpec((1, GROUP_W), lambda i: (0, 0)),
            pl.BlockSpec((1, GROUP_W), lambda i: (0, 0)),
            pl.BlockSpec((1, GROUP_W), lambda i: (0, 0)),
        ],
        out_specs=pl.BlockSpec((tl, GROUP_W), lambda i: (i, 0)),
        out_shape=jax.ShapeDtypeStruct((n, GROUP_W), BF16),
        scratch_shapes=[pltpu.VMEM((tl + 2 * CONV_HALO, GROUP_W), F32),
                        pltpu.VMEM((SUBLANES - 1, tl + 2 * CONV_HALO, GROUP_W), F32)],
        compiler_params=_params("parallel"),
        name="conformer_conv",
    )(u, u, u, w, b.reshape(1, -1), lg.reshape(1, -1), lb.reshape(1, -1))


def _fnet_ch_kernel(x_ref, w_ref, o_ref):
    o_ref[...] = jnp.dot(x_ref[...], w_ref[...], preferred_element_type=F32).astype(o_ref.dtype)


def _fnet_channels(u, wch, tm):
    n = u.shape[0]
    return pl.pallas_call(
        _fnet_ch_kernel,
        grid=(n // tm,),
        in_specs=[
            pl.BlockSpec((tm, 512), lambda i: (i, COL_FOUR // 512)),
            pl.BlockSpec((512, 1024), lambda i: (0, 0)),
        ],
        out_specs=pl.BlockSpec((tm, 1024), lambda i: (i, 0)),
        out_shape=jax.ShapeDtypeStruct((n, 1024), BF16),
        compiler_params=_params("parallel"),
        name="fnet_channels",
    )(u, wch)


def _fnet_pos_kernel(wc_ref, ws_ref, z_ref, o_ref):
    o = jnp.dot(wc_ref[...], z_ref[:, :512], preferred_element_type=F32)
    o = o + jnp.dot(ws_ref[...], z_ref[:, 512:], preferred_element_type=F32)
    o_ref[...] = o.astype(o_ref.dtype)


def _fnet_positions(z, wc, ws, row0, nb, lseq):
    tm = min(512, lseq)
    r0 = row0 // lseq
    return pl.pallas_call(
        _fnet_pos_kernel,
        grid=(nb, lseq // tm),
        in_specs=[
            pl.BlockSpec((tm, lseq), lambda b, i: (i, 0)),
            pl.BlockSpec((tm, lseq), lambda b, i: (i, 0)),
            pl.BlockSpec((lseq, 1024), lambda b, i: (r0 + b, 0)),
        ],
        out_specs=pl.BlockSpec((tm, 512), lambda b, i: (b * (lseq // tm) + i, 0)),
        out_shape=jax.ShapeDtypeStruct((nb * lseq, 512), BF16),
        compiler_params=_params("parallel", "arbitrary"),
        name="fnet_positions",
    )(wc, ws, z)


def _dft_tables(n):
    rb = DFT_ROW_BLOCK if n % DFT_ROW_BLOCK == 0 else 1
    t = jnp.arange(n, dtype=jnp.int32)[None, :]

    def trig(rows):
        ang = ((rows[:, None] * t) % n).astype(F32) * (2.0 * np.pi / n)
        return jnp.cos(ang), jnp.sin(ang)

    ca, sa = trig(jnp.arange(n // rb, dtype=jnp.int32) * rb)
    cb, sb = trig(jnp.arange(rb, dtype=jnp.int32))
    scale = n ** -0.5
    cos = (ca[:, None, :] * cb[None, :, :] - sa[:, None, :] * sb[None, :, :]).reshape(n, n) * scale
    sin = (sa[:, None, :] * cb[None, :, :] + ca[:, None, :] * sb[None, :, :]).reshape(n, n) * scale
    return cos, sin


def _outproj_kernel(ys_ref, yc_ref, ac_ref, al_ref, fc_ref, fl_ref, w_ref, x_ref, mod_ref, o_ref, *, n_ctx_blocks):
    def body(att_ref, four_ref):
        acc = jnp.dot(ys_ref[...], w_ref[0, 0], preferred_element_type=F32)
        acc = acc + jnp.dot(att_ref[...], w_ref[0, 1], preferred_element_type=F32)
        acc = acc + jnp.dot(yc_ref[...], w_ref[0, 2], preferred_element_type=F32)
        acc = acc + jnp.dot(four_ref[...], w_ref[0, 3], preferred_element_type=F32)
        o_ref[...] = x_ref[...] + mod_ref[0, 0][2:3] * acc

    is_ctx = pl.program_id(0) < n_ctx_blocks
    pl.when(is_ctx)(lambda: body(ac_ref, fc_ref))
    pl.when(jnp.logical_not(is_ctx))(lambda: body(al_ref, fl_ref))


def _outproj(y_ssd, y_conv, att_c, att_l, four_c, four_l, w_all, x, mod_all, layer, mod_row, tm):
    n, d = x.shape
    tn = 1024
    nbc = att_c.shape[0] // tm
    nbl = att_l.shape[0] // tm
    full = pl.BlockSpec((tm, GROUP_W), lambda i, j: (i, 0))
    ctx = pl.BlockSpec((tm, GROUP_W), lambda i, j: (jnp.minimum(i, nbc - 1), 0))
    lat = pl.BlockSpec((tm, GROUP_W), lambda i, j: (jnp.clip(i - nbc, 0, nbl - 1), 0))
    return pl.pallas_call(
        functools.partial(_outproj_kernel, n_ctx_blocks=nbc),
        grid=(n // tm, d // tn),
        in_specs=[full, full, ctx, lat, ctx, lat,
                  pl.BlockSpec((1, 4, GROUP_W, tn), lambda i, j: (layer, 0, 0, j)),
                  pl.BlockSpec((tm, tn), lambda i, j: (i, j)),
                  pl.BlockSpec((1, 1, N_MOD, tn), lambda i, j: (layer, mod_row(i), 0, j))],
        out_specs=pl.BlockSpec((tm, tn), lambda i, j: (i, j)),
        out_shape=jax.ShapeDtypeStruct((n, d), F32),
        compiler_params=_params("parallel", "arbitrary"),
        name="outproj",
    )(y_ssd, y_conv, att_c, att_l, four_c, four_l, w_all, x, mod_all)


ROUTE_LANE0 = N_EXPERT_GROUPS


def _route(lg):
    lane = lax.broadcasted_iota(jnp.int32, lg.shape, 1)
    lanef = lane.astype(F32)
    ninf = -jnp.inf
    isg = lane < N_EXPERT_GROUPS
    mg = jnp.max(jnp.where(isg, lg, ninf), axis=-1, keepdims=True)
    gsum = jnp.sum(jnp.where(isg, jnp.exp(jnp.where(isg, lg, ninf) - mg), 0.0), axis=-1, keepdims=True)
    gwt = 1.0 / gsum
    gi = jnp.min(jnp.where(isg, jnp.where(lg == mg, lanef, 1e9), 1e9), axis=-1, keepdims=True)
    grp = jnp.where(lane < ROUTE_LANE0 + N_EXPERTS, (lane - ROUTE_LANE0) // EXPERTS_PER_GROUP, -1).astype(F32)
    el1 = jnp.where(grp == gi, lg, ninf)
    v1 = jnp.max(el1, axis=-1, keepdims=True)
    i1 = jnp.min(jnp.where(el1 == v1, lanef, 1e9), axis=-1, keepdims=True)
    el2 = jnp.where(lanef == i1, ninf, el1)
    v2 = jnp.max(el2, axis=-1, keepdims=True)
    i2 = jnp.min(jnp.where(el2 == v2, lanef, 1e9), axis=-1, keepdims=True)
    t = jnp.exp(v2 - v1)
    ew1 = 1.0 / (1.0 + t)
    ew2 = t * ew1
    return gwt * (jnp.where(lanef == i1, ew1, 0.0) + jnp.where(lanef == i2, ew2, 0.0)), gi


def _moe_kernel(x_ref, g_ref, mod_ref, wr_ref, br_ref, w1_ref, w3_ref, w2_ref, o_ref,
                hs_scr, cs_scr, outs_scr, pt_scr, seg_smem, *, ne):
    e = pl.program_id(1)
    tm = x_ref.shape[0]
    tmp = hs_scr.shape[0]
    blk = MOE_BLOCK
    lane = lax.broadcasted_iota(jnp.int32, (1, LANES), 1)

    @pl.when(e == 0)
    def _():
        x = x_ref[...]
        ms = jnp.mean(x * x, axis=-1, keepdims=True)
        y = x * lax.rsqrt(ms + EPS) * g_ref[...]
        m = mod_ref[0, 0]
        h = y * (1.0 + m[4:5]) + m[3:4]
        hi, lo = _split_bf16(h)
        lg = (jnp.dot(hi, wr_ref[0, 0], preferred_element_type=F32)
              + jnp.dot(lo, wr_ref[0, 0], preferred_element_type=F32)
              + jnp.dot(hi, wr_ref[0, 1], preferred_element_type=F32)) + br_ref[0]
        comb, gi = _route(lg)
        onehot = lane.astype(F32) == gi
        ri = lax.broadcasted_iota(jnp.int32, (tm, tm), 0)
        ci = lax.broadcasted_iota(jnp.int32, (tm, tm), 1)
        earlier = jnp.where(ri > ci, 1.0, 0.0).astype(BF16)
        before = jnp.dot(earlier, jnp.where(onehot, 1.0, 0.0).astype(BF16), preferred_element_type=F32)
        rank = jnp.sum(jnp.where(onehot, before, 0.0), axis=-1, keepdims=True)
        cnt = jnp.sum(jnp.where(onehot, 1.0, 0.0), axis=0, keepdims=True)
        padded = jnp.floor((cnt + (blk - 1)) * (1.0 / blk)) * blk
        start = sum(pltpu.roll(padded, s, 1) for s in range(1, N_EXPERT_GROUPS))
        pos = rank + jnp.sum(jnp.where(onehot, start, 0.0), axis=-1, keepdims=True)
        col = lax.broadcasted_iota(jnp.int32, (tm, tmp), 1).astype(F32)
        pt_scr[...] = jnp.where(col == pos, 1.0, 0.0).astype(BF16)
        pos_row = jnp.broadcast_to(pos, (tm, LANES)).T[0:1, :]
        row = lax.broadcasted_iota(jnp.int32, (tmp, tm), 0).astype(F32)
        perm = jnp.where(row == pos_row, 1.0, 0.0).astype(BF16)
        hs_scr[...] = jnp.dot(perm, hi, preferred_element_type=F32).astype(BF16)
        chi, clo = _split_bf16(comb)
        cs_scr[...] = (jnp.dot(perm, chi, preferred_element_type=F32)
                       + jnp.dot(perm, clo, preferred_element_type=F32))
        outs_scr[...] = jnp.zeros_like(outs_scr)
        for g in range(N_EXPERT_GROUPS):
            seg_smem[g] = jnp.sum(jnp.where(lane == g, start, 0.0)).astype(jnp.int32)
            seg_smem[N_EXPERT_GROUPS + g] = jnp.sum(jnp.where(lane == g, padded, 0.0) * (1.0 / blk)).astype(jnp.int32)

    first = e * MOE_EXPERTS_PER_STEP
    grp = first // EXPERTS_PER_GROUP
    seg_start = seg_smem[grp]

    def block(j, carry):
        r = pl.multiple_of(seg_start + j * blk, blk)
        hb = hs_scr[pl.ds(r, blk), :]
        cw = cs_scr[pl.ds(r, blk), :]
        y = None
        for k in range(MOE_EXPERTS_PER_STEP):
            a = jnp.dot(hb, w1_ref[0, k], preferred_element_type=F32)
            b = jnp.dot(hb, w3_ref[0, k], preferred_element_type=F32)
            ce = jnp.sum(jnp.where(lane == first + k + ROUTE_LANE0, cw, 0.0), axis=-1, keepdims=True)
            hid = ((a * _sigmoid(a)) * b * ce).astype(BF16)
            yk = jnp.dot(hid, w2_ref[0, k], preferred_element_type=F32)
            y = yk if y is None else y + yk
        outs_scr[pl.ds(r, blk), :] += y
        return carry

    lax.fori_loop(0, seg_smem[N_EXPERT_GROUPS + grp], block, 0)

    @pl.when(e == ne // MOE_EXPERTS_PER_STEP - 1)
    def _():
        y = jnp.dot(pt_scr[...], outs_scr[...].astype(BF16), preferred_element_type=F32)
        o_ref[...] = x_ref[...] + mod_ref[0, 0][5:6] * y


def _moe(x, g, mod_all, wr, br, w1, w3, w2, layer, mod_row, tm):
    n, d = x.shape
    ne, ff = w1.shape[1], w1.shape[3]
    tmp = tm + N_EXPERT_GROUPS * MOE_BLOCK
    return pl.pallas_call(
        functools.partial(_moe_kernel, ne=ne),
        grid=(n // tm, ne // MOE_EXPERTS_PER_STEP),
        in_specs=[
            pl.BlockSpec((tm, d), lambda i, e: (i, 0)),
            pl.BlockSpec((1, d), lambda i, e: (0, 0)),
            pl.BlockSpec((1, 1, N_MOD, d), lambda i, e: (layer, mod_row(i), 0, 0)),
            pl.BlockSpec((1, 2, d, LANES), lambda i, e: (layer, 0, 0, 0)),
            pl.BlockSpec((1, 1, LANES), lambda i, e: (layer, 0, 0)),
            pl.BlockSpec((1, MOE_EXPERTS_PER_STEP, d, ff), lambda i, e: (layer, e, 0, 0)),
            pl.BlockSpec((1, MOE_EXPERTS_PER_STEP, d, ff), lambda i, e: (layer, e, 0, 0)),
            pl.BlockSpec((1, MOE_EXPERTS_PER_STEP, ff, d), lambda i, e: (layer, e, 0, 0)),
        ],
        out_specs=pl.BlockSpec((tm, d), lambda i, e: (i, 0)),
        out_shape=jax.ShapeDtypeStruct((n, d), F32),
        scratch_shapes=[pltpu.VMEM((tmp, d), BF16), pltpu.VMEM((tmp, LANES), F32), pltpu.VMEM((tmp, d), F32),
                        pltpu.VMEM((tm, tmp), BF16), pltpu.SMEM((2 * N_EXPERT_GROUPS,), jnp.int32)],
        compiler_params=_params("parallel", "arbitrary", vmem=VMEM_LIMIT_MOE),
        name="moe",
    )(x, g.reshape(1, d), mod_all, wr, br, w1, w3, w2)


def _fnorm_kernel(x_ref, g_ref, o_ref):
    x = x_ref[...]
    ms = jnp.mean(x * x, axis=-1, keepdims=True)
    o_ref[...] = x * lax.rsqrt(ms + EPS) * g_ref[...]


def _final_norm(x, g, row0, nrows, tm):
    d = x.shape[1]
    r0 = row0 // tm
    return pl.pallas_call(
        _fnorm_kernel,
        grid=(nrows // tm,),
        in_specs=[pl.BlockSpec((tm, d), lambda i: (r0 + i, 0)), pl.BlockSpec((1, d), lambda i: (0, 0))],
        out_specs=pl.BlockSpec((tm, d), lambda i: (i, 0)),
        out_shape=jax.ShapeDtypeStruct((nrows, d), F32),
        compiler_params=_params("parallel"),
        name="final_norm",
    )(x, g.reshape(1, d))


def _rope_tables(dec_seq):
    t = jnp.arange(dec_seq, dtype=jnp.int32)
    half = ROPE_AXIS_DIM // 2
    freqs = ROPE_THETA ** (-jnp.arange(half, dtype=F32) / half)

    def axis(pos):
        ang = pos.astype(F32)[:, None] * freqs[None, :]
        c, s = jnp.cos(ang), jnp.sin(ang)
        return jnp.concatenate([c, c], axis=-1), jnp.concatenate([-s, s], axis=-1)

    cr, sr = axis(t // GRID_W)
    cc, sc = axis(t % GRID_W)
    cos = jnp.tile(jnp.concatenate([cr, cc], axis=-1), (1, ATT_HEADS))
    sin = jnp.tile(jnp.concatenate([sr, sc], axis=-1), (1, ATT_HEADS))
    return cos, sin


def _block_diag_ones(width, block):
    i = np.arange(width) // block
    return jnp.asarray((i[:, None] == i[None, :]).astype(np.float32), dtype=BF16)


def _pack_w_in(w_in):
    depth, d, _ = w_in.shape
    seg = lambda a, b: w_in[:, :, a:b]
    zeros = lambda n: jnp.zeros((depth, d, n), w_in.dtype)
    parts = [seg(SRC_XBC, SRC_DT), seg(SRC_GLU, SRC_FOUR), seg(SRC_Z, SRC_XBC), seg(SRC_Q, SRC_K),
             seg(SRC_FOUR, SRC_END), seg(SRC_K, SRC_V), seg(SRC_V, SRC_GLU),
             seg(SRC_DT, SRC_DT + SSD_HEADS), zeros(LANES - SSD_HEADS),
             seg(SRC_DT + SSD_HEADS, SRC_Q), zeros(LANES - SSD_HEADS)]
    return jnp.concatenate(parts, axis=-1).astype(BF16)


def _lanes8(v):
    return jnp.pad(v.astype(F32), ((0, 0), (0, 0), (0, LANES - SSD_HEADS)))[:, :, None, :]


def kernel(x_prompt, x_sample, cache_k, cache_v, state_ssd, c, c_ctx, norm1_g, norm2_g, w_mod, b_mod, w_in,
           ssd_conv_w, ssd_conv_b, ssd_dt_bias, ssd_A_log, ssd_D, ssd_norm_g, q_norm_g, k_norm_g, cf_dw_w, cf_dw_b,
           cf_ln_g, cf_ln_b, w_out, router_group_w, router_group_b, router_expert_w, router_expert_b, w1, w3, w2,
           final_norm_g):
    bc, lc, d = x_prompt.shape
    bl, ll, _ = x_sample.shape
    n_ctx, n_lat = bc * lc, bl * ll
    n = n_ctx + n_lat
    depth = w_in.shape[0]
    kvw = ATT_KV_HEADS * HEAD_DIM
    assert lc & (lc - 1) == 0 and ll & (ll - 1) == 0 and lc % SSD_CHUNK == 0 and ll % SSD_CHUNK == 0
    assert n_ctx % ll == 0 and ll % GRID_W == 0

    tm = _pow2_tile(1024, n_ctx, ll)
    tm_moe = _pow2_tile(512, n_ctx, ll)
    tm_qk = _pow2_tile(512, lc, ll)

    def mod_row_fn(t):
        nb, per = n_ctx // t, ll // t
        return lambda i: jnp.where(i < nb, 0, 1 + (i - nb) // per)

    w_in_p = _pack_w_in(w_in)
    w_out_p = w_out.astype(BF16).reshape(depth, 4, GROUP_W, d)
    w1_b, w3_b, w2_b = w1.astype(BF16), w3.astype(BF16), w2.astype(BF16)
    wr = jnp.concatenate([router_group_w, router_expert_w,
                          jnp.zeros((depth, d, LANES - N_EXPERT_GROUPS - N_EXPERTS), F32)], axis=-1)
    wr_hi = wr.astype(BF16)
    wr_lo = (wr - wr_hi.astype(F32)).astype(BF16)
    wr_p = jnp.stack([wr_hi, wr_lo], axis=1)
    br_p = jnp.concatenate([router_group_b, router_expert_b,
                            jnp.zeros((depth, LANES - N_EXPERT_GROUPS - N_EXPERTS), F32)], axis=-1)[:, None, :]
    dt_bias_p = _lanes8(ssd_dt_bias)
    a_log_p = _lanes8(ssd_A_log)
    d_lanes = jnp.repeat(ssd_D.astype(F32), SSD_HEADDIM, axis=-1)[:, None, :]
    expand_np = np.arange(LANES)[:, None] == (np.arange(512)[None, :] // SSD_HEADDIM)
    expand = jnp.asarray(expand_np, dtype=BF16)
    expand_t = jnp.asarray(expand_np.T, dtype=BF16)
    gq = jnp.tile(q_norm_g, (1, ATT_HEADS))[:, None, :]
    gk = jnp.tile(k_norm_g, (1, ATT_KV_HEADS))[:, None, :]
    bdq = _block_diag_ones(512, HEAD_DIM)
    bdk = _block_diag_ones(kvw, HEAD_DIM)
    rope_tabs = _rope_tables(ll)
    cch, sch = _dft_tables(FNET_GROUP_CH)
    eye4 = jnp.eye(4, dtype=F32)
    wch = jnp.concatenate([jnp.kron(eye4, cch), jnp.kron(eye4, sch)], axis=-1).astype(BF16)
    dft_c = {}
    for lseq in (lc, ll):
        cl, sl = _dft_tables(lseq)
        dft_c[lseq] = (cl.astype(BF16), (-sl).astype(BF16))
    kz_cache = _cache_variants(cache_k, ones_tail=False)
    vz_cache = _cache_variants(cache_v, ones_tail=True)

    mod_all = _adaln(jnp.concatenate([c_ctx[None, :], c], axis=0), w_mod, b_mod)

    x = jnp.concatenate([x_prompt.reshape(n_ctx, d), x_sample.reshape(n_lat, d)], axis=0)
    h0_ctx = jnp.zeros((bc, 2, 512, SSD_D_STATE), F32)
    h0_lat = state_ssd.reshape(bl, depth, 2, 512, SSD_D_STATE)
    ks, vs, ss = [], [], []
    for l in range(depth):
        u = _inproj(x, norm1_g[l], mod_all, w_in_p, l, mod_row_fn(tm), tm)

        xbc = _ssdpre(u, ssd_conv_w[l], ssd_conv_b[l], n_ctx, lc, ll)
        yf_c, yb_c, h_c = _ssd_scan(xbc, u, dt_bias_p[l], a_log_p[l], expand, expand_t, h0_ctx, 0, bc, lc)
        yf_l, yb_l, _ = _ssd_scan(xbc, u, dt_bias_p[l], a_log_p[l], expand, expand_t, h0_lat[:, l], n_ctx, bl, ll)
        y_ssd = _gnorm((yf_c, yb_c), (yf_l, yb_l), xbc, u, d_lanes[l], ssd_norm_g[l], tm)
        ss.append(h_c.reshape(bc, 2, SSD_HEADS, SSD_HEADDIM, SSD_D_STATE))

        q_c, kz_c, vz_c, k_c, v_c = _qk_prep(u, gq[l], gk[l], bdq, bdk, 0, bc, lc, tm_qk, emit_kv=True)
        q_l, kz_l, vz_l = _qk_prep(u, gq[l], gk[l], bdq, bdk, n_ctx, bl, ll, tm_qk, tables=rope_tabs)
        ks.append(k_c.reshape(bc, lc, ATT_KV_HEADS, HEAD_DIM))
        vs.append(v_c.reshape(bc, lc, ATT_KV_HEADS, HEAD_DIM))
        att_c = _attention(q_c, kz_c, vz_c, min(256, lc))
        att_l = _attention(q_l, kz_l, vz_l, min(256, ll), cache=(kz_cache, vz_cache, l))

        y_conv = _conformer(u, cf_dw_w[l], cf_dw_b[l], cf_ln_g[l], cf_ln_b[l], n_ctx, lc, ll)

        zf = _fnet_channels(u, wch, tm)
        four_c = _fnet_positions(zf, *dft_c[lc], 0, bc, lc)
        four_l = _fnet_positions(zf, *dft_c[ll], n_ctx, bl, ll)

        x = _outproj(y_ssd, y_conv, att_c, att_l, four_c, four_l, w_out_p, x, mod_all, l, mod_row_fn(tm), tm)
        x = _moe(x, norm2_g[l], mod_all, wr_p, br_p, w1_b, w3_b, w2_b, l, mod_row_fn(tm_moe), tm_moe)

    y_prompt = _final_norm(x, final_norm_g, 0, n_ctx, tm).reshape(bc, lc, d)
    y_sample = _final_norm(x, final_norm_g, n_ctx, n_lat, tm).reshape(bl, ll, d)
    return (y_prompt, y_sample, jnp.stack(ks, axis=1), jnp.stack(vs, axis=1), jnp.stack(ss, axis=1))
```

```python
import functools

import numpy as np
import jax
import jax.numpy as jnp
from jax import lax
from jax.experimental import pallas as pl
from jax.experimental.pallas import tpu as pltpu

F32 = jnp.float32
BF16 = jnp.bfloat16
HIGHEST = lax.Precision.HIGHEST

D_MODEL = 2048
DEPTH = 4
GRID_W = 64
GROUP_W = 512
SSD_HEADS = 8
SSD_HEADDIM = 64
SSD_GROUPS = 2
SSD_D_STATE = 128
SSD_CHUNK = 128
HEAD_DIM = 64
ATT_HEADS = 8
ATT_KV_HEADS = 2
ROPE_THETA = 10000.0
ROPE_AXIS_DIM = 32
CONV_WIDTH = 31
CONV_HALO = 16
SUBLANES = 8
FNET_GROUP_CH = 128
DFT_ROW_BLOCK = 64
N_EXPERT_GROUPS = 4
EXPERTS_PER_GROUP = 4
N_EXPERTS = 16
EXPERT_FF = 256
N_MOD = 6
EPS = 1e-6
LANES = 128

U_WIDTH = 4096
COL_XBC, COL_GLU, COL_Z, COL_Q, COL_FOUR, COL_K, COL_V, COL_DTF, COL_DTB = (
    0, 1024, 2048, 2560, 3072, 3584, 3712, 3840, 3968)
SRC_Z, SRC_XBC, SRC_DT, SRC_Q, SRC_K, SRC_V, SRC_GLU, SRC_FOUR, SRC_END = (
    0, 512, 1536, 1552, 2064, 2192, 2320, 3344, 3856)

VMEM_LIMIT = 48 * 1024 * 1024
VMEM_LIMIT_MOE = 54 * 1024 * 1024

NT_DIMS = (((1,), (1,)), ((), ()))
Q_SCALE = HEAD_DIM ** -0.5 * float(np.log2(np.e))
ATT_KEY_CHUNK = 1024
MOE_BLOCK = 128
MOE_EXPERTS_PER_STEP = 2
assert EXPERTS_PER_GROUP % MOE_EXPERTS_PER_STEP == 0
SSD_SEQ_BLOCK = 2


def _params(*sem, vmem=VMEM_LIMIT):
    return pltpu.CompilerParams(dimension_semantics=sem, vmem_limit_bytes=vmem)


def _sigmoid(x):
    return 1.0 / (1.0 + jnp.exp(-x))


def _split_bf16(x):
    hi = x.astype(BF16)
    lo = (x - hi.astype(F32)).astype(BF16)
    return hi, lo


def _pow2_tile(pref, *dims):
    t = pref
    while any(d % t for d in dims):
        t //= 2
    return t


def _mod_kernel(c_ref, w_ref, b_ref, o_ref, acc_ref, *, nk):
    k = pl.program_id(2)

    @pl.when(k == 0)
    def _():
        acc_ref[...] = jnp.zeros_like(acc_ref)

    w = w_ref[0]
    tk, tn = w.shape
    for r in range(c_ref.shape[0]):
        c = c_ref[r]
        s = c * _sigmoid(c)
        acc_ref[r] += (s * w).reshape(tk // SUBLANES, SUBLANES, tn).sum(axis=0)

    @pl.when(k == nk - 1)
    def _():
        o_ref[0] = acc_ref[...].sum(axis=1) + b_ref[0]


def _adaln(cvec, w_mod, b_mod):
    r, d = cvec.shape
    depth, _, n = w_mod.shape
    tk, tn = 1024, 2048
    nk = d // tk
    out = pl.pallas_call(
        functools.partial(_mod_kernel, nk=nk),
        grid=(depth, n // tn, nk),
        in_specs=[
            pl.BlockSpec((r, tk, 1), lambda l, j, k: (0, k, 0)),
            pl.BlockSpec((1, tk, tn), lambda l, j, k: (l, k, j)),
            pl.BlockSpec((1, 1, tn), lambda l, j, k: (l, 0, j)),
        ],
        out_specs=pl.BlockSpec((1, r, tn), lambda l, j, k: (l, 0, j)),
        out_shape=jax.ShapeDtypeStruct((depth, r, n), F32),
        scratch_shapes=[pltpu.VMEM((r, SUBLANES, tn), F32)],
        compiler_params=_params("parallel", "parallel", "arbitrary"),
        name="adaln",
    )(cvec.reshape(r, d, 1), w_mod, b_mod.reshape(depth, 1, n))
    return out.reshape(depth, r, N_MOD, d)


def _inproj_kernel(x_ref, g_ref, mod_ref, w_ref, o_ref):
    x = x_ref[...]
    ms = jnp.mean(x * x, axis=-1, keepdims=True)
    y = x * lax.rsqrt(ms + EPS) * g_ref[...]
    m = mod_ref[0, 0]
    h = (y * (1.0 + m[1:2]) + m[0:1]).astype(BF16)
    o_ref[...] = jnp.dot(h, w_ref[0], preferred_element_type=F32).astype(o_ref.dtype)


def _inproj(x, g, mod_all, w_all, layer, mod_row, tm):
    n, d = x.shape
    nout = w_all.shape[2]
    tn = 2048
    return pl.pallas_call(
        _inproj_kernel,
        grid=(n // tm, nout // tn),
        in_specs=[
            pl.BlockSpec((tm, d), lambda i, j: (i, 0)),
            pl.BlockSpec((1, d), lambda i, j: (0, 0)),
            pl.BlockSpec((1, 1, N_MOD, d), lambda i, j: (layer, mod_row(i), 0, 0)),
            pl.BlockSpec((1, d, tn), lambda i, j: (layer, 0, j)),
        ],
        out_specs=pl.BlockSpec((tm, tn), lambda i, j: (i, j)),
        out_shape=jax.ShapeDtypeStruct((n, nout), BF16),
        compiler_params=_params("parallel", "arbitrary"),
        name="inproj",
    )(x, g.reshape(1, d), mod_all, w_all)


def _ssdpre_kernel(x_ref, w_ref, b_ref, o_ref, *, n_ctx_blocks, seq, dec_seq):
    x = x_ref[...].astype(F32)
    tl = x.shape[0]
    lseq = jnp.where(pl.program_id(0) < n_ctx_blocks, seq, dec_seq)
    pos = lax.broadcasted_iota(jnp.int32, (tl, 1), 0) & (lseq - 1)
    xp = jnp.where(pos == 0, 0.0, pltpu.roll(x, 1, 0))
    xn = jnp.where(pos == lseq - 1, 0.0, pltpu.roll(x, tl - 1, 0))
    w = w_ref[...]
    y = xp * w[0:1] + x * w[1:2] + xn * w[2:3] + b_ref[...]
    o_ref[...] = (y * _sigmoid(y)).astype(o_ref.dtype)


def _ssdpre(u, w, b, n_ctx, seq, dec_seq):
    n = u.shape[0]
    c = w.shape[1]
    tl = max(seq, dec_seq)
    assert n_ctx % tl == 0 and tl % seq == 0 and tl % dec_seq == 0
    tc = 256
    return pl.pallas_call(
        functools.partial(_ssdpre_kernel, n_ctx_blocks=n_ctx // tl, seq=seq, dec_seq=dec_seq),
        grid=(n // tl, c // tc),
        in_specs=[
            pl.BlockSpec((tl, tc), lambda i, j: (i, j + COL_XBC // tc)),
            pl.BlockSpec((3, tc), lambda i, j: (0, j)),
            pl.BlockSpec((1, tc), lambda i, j: (0, j)),
        ],
        out_specs=pl.BlockSpec((tl, tc), lambda i, j: (i, j)),
        out_shape=jax.ShapeDtypeStruct((n, c), BF16),
        compiler_params=_params("parallel", "parallel"),
        name="ssd_conv",
    )(u, w, b.reshape(1, c))


def _ssd_direction(d, x_ref, bc_ref, dt_ref, bias, alog, e, et, h_scr, y_ref):
    q = x_ref.shape[0]
    hpg = SSD_HEADS // SSD_GROUPS
    gw = hpg * SSD_HEADDIM
    x = x_ref[...].astype(F32)
    bc = bc_ref[...]
    raw = dt_ref[...].astype(F32) + bias
    dt = jnp.maximum(raw, 0.0) + jnp.log1p(jnp.exp(-jnp.abs(raw)))
    a = dt * (-jnp.exp(alog))
    ri = lax.broadcasted_iota(jnp.int32, (q, q), 0)
    ci = lax.broadcasted_iota(jnp.int32, (q, q), 1)
    tri = (ri >= ci).astype(F32)
    cs = jnp.dot(tri, a, precision=HIGHEST, preferred_element_type=F32)
    total = cs[q - 1:q, :]
    csa = cs if d == 0 else total - cs + a
    mask = (ri >= ci) if d == 0 else (ri <= ci)
    lhs = jnp.concatenate([dt * jnp.exp(total - csa), jnp.exp(csa)], axis=0)
    hi, lo = _split_bf16(lhs)
    ex = jnp.dot(hi, e, preferred_element_type=F32) + jnp.dot(lo, e, preferred_element_type=F32)
    w_x = ex[0:q]
    ecs_x = ex[q:2 * q]
    cd_t = jnp.exp(jnp.broadcast_to(total, (LANES, LANES)).T)
    hi, lo = _split_bf16(cd_t)
    cd_rows = jnp.dot(et, hi, preferred_element_type=F32) + jnp.dot(et, lo, preferred_element_type=F32)
    csa_t = csa.T
    dt_t = dt.T
    xb = x_ref[...]
    xw = x * w_x
    lane_head = lax.broadcasted_iota(jnp.int32, (1, gw), 1) // SSD_HEADDIM
    for g in range(SSD_GROUPS):
        bm = bc[:, g * SSD_D_STATE:(g + 1) * SSD_D_STATE]
        cm = bc[:, (SSD_GROUPS + g) * SSD_D_STATE:(SSD_GROUPS + g + 1) * SSD_D_STATE]
        cb = lax.dot_general(cm, bm, NT_DIMS, preferred_element_type=F32)
        xg = xb[:, g * gw:(g + 1) * gw]
        h_g = h_scr[g * gw:(g + 1) * gw, :]
        y_g = lax.dot_general(cm, h_g.astype(BF16), NT_DIMS, preferred_element_type=F32)
        y_g = y_g * ecs_x[:, g * gw:(g + 1) * gw]
        for hh in range(hpg):
            h = g * hpg + hh
            seg = csa[:, h:h + 1] - csa_t[h:h + 1, :]
            decay = jnp.exp(jnp.where(mask, seg, -jnp.inf))
            sc = (cb * decay * dt_t[h:h + 1, :]).astype(BF16)
            xm = jnp.where(lane_head == hh, xg, jnp.zeros_like(xg))
            y_g = y_g + jnp.dot(sc, xm, preferred_element_type=F32)
        xw_t = xw[:, g * gw:(g + 1) * gw].T.astype(BF16)
        st = jnp.dot(xw_t, bm, preferred_element_type=F32)
        y_ref[:, g * gw:(g + 1) * gw] = y_g.astype(y_ref.dtype)
        h_scr[g * gw:(g + 1) * gw, :] = cd_rows[g * gw:(g + 1) * gw, :] * h_g + st


def _ssd_kernel(xf_ref, bcf_ref, dtf_ref, xb_ref, bcb_ref, dtb_ref, bias_ref, alog_ref, e_ref, et_ref, h0_ref,
                yf_ref, yb_ref, hout_ref, h_scr, *, nc):
    c = pl.program_id(1)

    @pl.when(c == 0)
    def _():
        h_scr[...] = h0_ref[...]

    e = e_ref[...]
    et = et_ref[...]
    for s in range(xf_ref.shape[0]):
        _ssd_direction(0, xf_ref.at[s], bcf_ref.at[s], dtf_ref.at[s], bias_ref[0], alog_ref[0], e, et,
                       h_scr.at[s, 0], yf_ref.at[s])
        _ssd_direction(1, xb_ref.at[s], bcb_ref.at[s], dtb_ref.at[s], bias_ref[1], alog_ref[1], e, et,
                       h_scr.at[s, 1], yb_ref.at[s])

    @pl.when(c == nc - 1)
    def _():
        hout_ref[...] = h_scr[...]


def _ssd_scan(xbc, u, bias, alog, expand, expand_t, h0, row0, nb, lseq):
    q = SSD_CHUNK
    nc = lseq // q
    n = xbc.shape[0]
    s0 = row0 // lseq
    sb = SSD_SEQ_BLOCK if (nb % SSD_SEQ_BLOCK == 0 and s0 % SSD_SEQ_BLOCK == 0) else 1
    xbc3 = xbc.reshape(n // lseq, lseq, xbc.shape[1])
    u3 = u.reshape(n // lseq, lseq, u.shape[1])
    sq = lambda b: s0 // sb + b
    sspec = pl.BlockSpec((sb, 2, 512, SSD_D_STATE), lambda b, c: (b, 0, 0, 0))
    yf, yb, hout = pl.pallas_call(
        functools.partial(_ssd_kernel, nc=nc),
        grid=(nb // sb, nc),
        in_specs=[
            pl.BlockSpec((sb, q, 512), lambda b, c: (sq(b), c, 0)),
            pl.BlockSpec((sb, q, 512), lambda b, c: (sq(b), c, 1)),
            pl.BlockSpec((sb, q, LANES), lambda b, c: (sq(b), c, COL_DTF // LANES)),
            pl.BlockSpec((sb, q, 512), lambda b, c: (sq(b), nc - 1 - c, 0)),
            pl.BlockSpec((sb, q, 512), lambda b, c: (sq(b), nc - 1 - c, 1)),
            pl.BlockSpec((sb, q, LANES), lambda b, c: (sq(b), nc - 1 - c, COL_DTB // LANES)),
            pl.BlockSpec((2, 1, LANES), lambda b, c: (0, 0, 0)),
            pl.BlockSpec((2, 1, LANES), lambda b, c: (0, 0, 0)),
            pl.BlockSpec((LANES, 512), lambda b, c: (0, 0)),
            pl.BlockSpec((512, LANES), lambda b, c: (0, 0)),
            sspec,
        ],
        out_specs=[
            pl.BlockSpec((sb, q, 512), lambda b, c: (b, c, 0)),
            pl.BlockSpec((sb, q, 512), lambda b, c: (b, nc - 1 - c, 0)),
            sspec,
        ],
        out_shape=[
            jax.ShapeDtypeStruct((nb, lseq, 512), BF16),
            jax.ShapeDtypeStruct((nb, lseq, 512), BF16),
            jax.ShapeDtypeStruct((nb, 2, 512, SSD_D_STATE), F32),
        ],
        scratch_shapes=[pltpu.VMEM((sb, 2, 512, SSD_D_STATE), F32)],
        compiler_params=_params("parallel", "arbitrary"),
        name="ssd_scan",
    )(xbc3, xbc3, u3, xbc3, xbc3, u3, bias, alog, expand, expand_t, h0)
    return yf.reshape(nb * lseq, 512), yb.reshape(nb * lseq, 512), hout


def _gnorm_kernel(yfc_ref, ybc_ref, yfl_ref, ybl_ref, x_ref, z_ref, dl_ref, g_ref, o_ref, *, n_ctx_blocks):
    def body(yf_ref, yb_ref):
        y = yf_ref[...].astype(F32) + yb_ref[...].astype(F32) + x_ref[...].astype(F32) * dl_ref[...]
        z = z_ref[...].astype(F32)
        t = y * (z * _sigmoid(z))
        ms = jnp.mean(t * t, axis=-1, keepdims=True)
        o_ref[...] = (t * lax.rsqrt(ms + EPS) * g_ref[...]).astype(o_ref.dtype)

    is_ctx = pl.program_id(0) < n_ctx_blocks
    pl.when(is_ctx)(lambda: body(yfc_ref, ybc_ref))
    pl.when(jnp.logical_not(is_ctx))(lambda: body(yfl_ref, ybl_ref))


def _gnorm(y_c, y_l, xbc, u, d_lanes, g, tm):
    n = xbc.shape[0]
    nbc = y_c[0].shape[0] // tm
    nbl = y_l[0].shape[0] // tm
    ctx = pl.BlockSpec((tm, 512), lambda i: (jnp.minimum(i, nbc - 1), 0))
    lat = pl.BlockSpec((tm, 512), lambda i: (jnp.clip(i - nbc, 0, nbl - 1), 0))
    return pl.pallas_call(
        functools.partial(_gnorm_kernel, n_ctx_blocks=nbc),
        grid=(n // tm,),
        in_specs=[
            ctx, ctx, lat, lat,
            pl.BlockSpec((tm, 512), lambda i: (i, 0)),
            pl.BlockSpec((tm, 512), lambda i: (i, COL_Z // 512)),
            pl.BlockSpec((1, 512), lambda i: (0, 0)),
            pl.BlockSpec((1, 512), lambda i: (0, 0)),
        ],
        out_specs=pl.BlockSpec((tm, 512), lambda i: (i, 0)),
        out_shape=jax.ShapeDtypeStruct((n, 512), BF16),
        compiler_params=_params("parallel"),
        name="ssd_gate_norm",
    )(*y_c, *y_l, xbc, u, d_lanes, g.reshape(1, 512))


def _head_rms(t, bd, gain):
    hi, lo = _split_bf16(t * t)
    ss = jnp.dot(hi, bd, preferred_element_type=F32) + jnp.dot(lo, bd, preferred_element_type=F32)
    return t * lax.rsqrt(ss * (1.0 / HEAD_DIM) + EPS) * gain


def _rope(t, cos, sin):
    w = t.shape[1]
    lane = lax.broadcasted_iota(jnp.int32, (1, w), 1)
    first = (lane & (ROPE_AXIS_DIM - 1)) < ROPE_AXIS_DIM // 2
    partner = jnp.where(first, pltpu.roll(t, w - ROPE_AXIS_DIM // 2, 1), pltpu.roll(t, ROPE_AXIS_DIM // 2, 1))
    return t * cos + partner * sin


def _dup_variants(t, o_ref, ones_tail):
    w = 2 * HEAD_DIM
    lane = lax.broadcasted_iota(jnp.int32, (1, w), 1)
    lo_half = lane < HEAD_DIM
    sw = pltpu.roll(t, HEAD_DIM, 1)
    o_ref[0, 0, :, 0:w] = jnp.where(lo_half, t, sw).astype(o_ref.dtype)
    o_ref[0, 1, :, 0:w] = jnp.where(lo_half, sw, t).astype(o_ref.dtype)
    if ones_tail:
        ones = jnp.ones(t.shape, o_ref.dtype)
        o_ref[0, 0, :, w:2 * w] = ones
        o_ref[0, 1, :, w:2 * w] = ones


def _qk_kernel(*refs, rope, emit_kv):
    q_ref, k_ref, v_ref, gq_ref, gk_ref, bdq_ref, bdk_ref = refs[:7]
    refs = refs[7:]
    if rope:
        cos_ref, sin_ref, cosk_ref, sink_ref = refs[:4]
        refs = refs[4:]
    qo_ref, kz_ref, vz_ref = refs[:3]
    qn = _head_rms(q_ref[...].astype(F32), bdq_ref[...], gq_ref[...])
    kn = _head_rms(k_ref[...].astype(F32), bdk_ref[...], gk_ref[...])
    v = v_ref[...].astype(F32)
    if emit_kv:
        ko_ref, vo_ref = refs[3:5]
        ko_ref[...] = kn
        vo_ref[...] = v
    if rope:
        qn = _rope(qn, cos_ref[...], sin_ref[...])
        kn = _rope(kn, cosk_ref[...], sink_ref[...])
    qo_ref[...] = (qn * Q_SCALE).astype(qo_ref.dtype)
    _dup_variants(kn, kz_ref, ones_tail=False)
    _dup_variants(v, vz_ref, ones_tail=True)


def _qk_prep(u, gq, gk, bdq, bdk, row0, nb, lseq, tm, tables=None, emit_kv=False):
    kw = ATT_KV_HEADS * HEAD_DIM
    nrows = nb * lseq
    r0 = row0 // tm
    per = lseq // tm
    in_specs = [
        pl.BlockSpec((tm, 512), lambda i: (r0 + i, COL_Q // 512)),
        pl.BlockSpec((tm, kw), lambda i: (r0 + i, COL_K // kw)),
        pl.BlockSpec((tm, kw), lambda i: (r0 + i, COL_V // kw)),
        pl.BlockSpec((1, 512), lambda i: (0, 0)),
        pl.BlockSpec((1, kw), lambda i: (0, 0)),
        pl.BlockSpec((512, 512), lambda i: (0, 0)),
        pl.BlockSpec((kw, kw), lambda i: (0, 0)),
    ]
    args = [u, u, u, gq, gk, bdq, bdk]
    if tables is not None:
        cos, sin = tables
        in_specs += [
            pl.BlockSpec((tm, 512), lambda i: (i % per, 0)),
            pl.BlockSpec((tm, 512), lambda i: (i % per, 0)),
            pl.BlockSpec((tm, kw), lambda i: (i % per, 0)),
            pl.BlockSpec((tm, kw), lambda i: (i % per, 0)),
        ]
        args += [cos, sin, cos, sin]
    zspec = pl.BlockSpec((1, ATT_KV_HEADS, tm, kw), lambda i: (i // per, 0, i % per, 0))
    out_specs = [pl.BlockSpec((tm, 512), lambda i: (i, 0)), zspec, zspec]
    vspec = pl.BlockSpec((1, ATT_KV_HEADS, tm, 2 * kw), lambda i: (i // per, 0, i % per, 0))
    out_specs[2] = vspec
    out_shape = [jax.ShapeDtypeStruct((nrows, 512), BF16),
                 jax.ShapeDtypeStruct((nb, ATT_KV_HEADS, lseq, kw), BF16),
                 jax.ShapeDtypeStruct((nb, ATT_KV_HEADS, lseq, 2 * kw), BF16)]
    if emit_kv:
        out_specs += [pl.BlockSpec((tm, kw), lambda i: (i, 0))] * 2
        out_shape += [jax.ShapeDtypeStruct((nrows, kw), F32)] * 2
    return pl.pallas_call(
        functools.partial(_qk_kernel, rope=tables is not None, emit_kv=emit_kv),
        grid=(nrows // tm,),
        in_specs=in_specs,
        out_specs=out_specs,
        out_shape=out_shape,
        compiler_params=_params("parallel"),
        name="qk_norm_rope" if tables is not None else "qk_norm",
    )(*args)


def _attn_kernel(*refs, cached):
    if cached:
        q_ref, kn_ref, vn_ref, kc_ref, vc_ref, o_ref = refs
    else:
        q_ref, kn_ref, vn_ref, o_ref = refs
    pair_w = 2 * HEAD_DIM
    tq = q_ref.shape[0]
    ck = min(ATT_KEY_CHUNK, kn_ref.shape[2])
    lo_half = lax.broadcasted_iota(jnp.int32, (1, pair_w), 1) < HEAD_DIM
    for pair in range(ATT_HEADS // 2):
        kv = pair // (ATT_HEADS // ATT_KV_HEADS // 2)
        qp = q_ref[:, pair * pair_w:(pair + 1) * pair_w]
        zero = jnp.zeros_like(qp)
        q2 = jnp.concatenate([jnp.where(lo_half, qp, zero), jnp.where(lo_half, zero, qp)], axis=0)
        chunks = [(kn_ref.at[0, kv], vn_ref.at[0, kv], j * ck, ck) for j in range(kn_ref.shape[2] // ck)]
        if cached:
            chunks.append((kc_ref.at[0, 0, kv], vc_ref.at[0, 0, kv], 0, kc_ref.shape[3]))
        parts = []
        for k_ref, v_ref, start, size in chunks:
            s = lax.dot_general(q2, k_ref[start:start + size, :], NT_DIMS, preferred_element_type=F32)
            m_c = jnp.max(s, axis=-1, keepdims=True)
            o_c = jnp.dot(jnp.exp2(s - m_c).astype(BF16), v_ref[start:start + size, :], preferred_element_type=F32)
            parts.append((m_c, o_c))
        m = functools.reduce(jnp.maximum, [m_c for m_c, _ in parts])
        o = sum(o_c * jnp.exp2(m_c - m) for m_c, o_c in parts)
        o = o[:, :pair_w] / o[:, pair_w:]
        o_ref[:, pair * pair_w:(pair + 1) * pair_w] = jnp.where(lo_half, o[:tq], o[tq:]).astype(o_ref.dtype)


def _attention(q, kz, vz, tq, cache=None):
    nb, nkv, l, kw = kz.shape
    per = l // tq
    in_specs = [
        pl.BlockSpec((tq, 512), lambda b, i: (b * per + i, 0)),
        pl.BlockSpec((1, nkv, l, kw), lambda b, i: (b, 0, 0, 0)),
        pl.BlockSpec((1, nkv, l, 2 * kw), lambda b, i: (b, 0, 0, 0)),
    ]
    args = [q, kz, vz]
    if cache is not None:
        kzc, vzc, layer = cache
        past = kzc.shape[3]
        in_specs += [pl.BlockSpec((1, 1, nkv, past, kw), lambda b, i: (b, layer, 0, 0, 0)),
                     pl.BlockSpec((1, 1, nkv, past, 2 * kw), lambda b, i: (b, layer, 0, 0, 0))]
        args += [kzc, vzc]
    return pl.pallas_call(
        functools.partial(_attn_kernel, cached=cache is not None),
        grid=(nb, per),
        in_specs=in_specs,
        out_specs=pl.BlockSpec((tq, 512), lambda b, i: (b * per + i, 0)),
        out_shape=jax.ShapeDtypeStruct((nb * l, 512), BF16),
        compiler_params=_params("parallel", "parallel"),
        name="attention_cached" if cache is not None else "attention",
    )(*args)


def _cache_variants(t, ones_tail):
    tb = t.astype(BF16).transpose(0, 1, 3, 2, 4)
    parts = [tb, tb] + ([jnp.ones(tb.shape[:-1] + (2 * HEAD_DIM,), BF16)] if ones_tail else [])
    return jnp.concatenate(parts, axis=-1)


def _glu(t):
    t = t.astype(F32)
    return t[:, :GROUP_W] * _sigmoid(t[:, GROUP_W:])


def _conf_kernel(m_ref, p_ref, n_ref, w_ref, b_ref, lg_ref, lb_ref, o_ref, scr, sh_scr, *, n_ctx_blocks, seq,
                 dec_seq):
    i = pl.program_id(0)
    tl = m_ref.shape[0]
    rows = tl + 2 * CONV_HALO
    lseq = jnp.where(i < n_ctx_blocks, seq, dec_seq)
    has_prev = ((i * tl) & (lseq - 1)) != 0
    has_next = (((i + 1) * tl) & (lseq - 1)) != 0
    scr[0:CONV_HALO, :] = jnp.where(has_prev, _glu(p_ref[...]), 0.0)
    scr[CONV_HALO:CONV_HALO + tl, :] = _glu(m_ref[...])
    scr[CONV_HALO + tl:rows, :] = jnp.where(has_next, _glu(n_ref[...]), 0.0)
    full = scr[...]
    for b in range(1, SUBLANES):
        sh_scr[b - 1] = pltpu.roll(full, rows - b, 0)
    acc = jnp.zeros((tl, GROUP_W), F32) + b_ref[...]
    off = CONV_HALO - CONV_WIDTH // 2
    for k in range(CONV_WIDTH):
        a, b = divmod(off + k, SUBLANES)
        src = scr if b == 0 else sh_scr.at[b - 1]
        acc = acc + src[a * SUBLANES:a * SUBLANES + tl, :] * w_ref[k:k + 1, :]
    mu = jnp.mean(acc, axis=-1, keepdims=True)
    xc = acc - mu
    var = jnp.mean(xc * xc, axis=-1, keepdims=True)
    y = xc * lax.rsqrt(var + EPS) * lg_ref[...] + lb_ref[...]
    o_ref[...] = (y * _sigmoid(y)).astype(o_ref.dtype)


def _conformer(u, w, b, lg, lb, n_ctx, seq, dec_seq):
    n = u.shape[0]
    tl = _pow2_tile(256, seq, dec_seq)
    hb = tl // CONV_HALO
    nhb = n // CONV_HALO
    gcol = COL_GLU // (2 * GROUP_W)
    return pl.pallas_call(
        functools.partial(_conf_kernel, n_ctx_blocks=n_ctx // tl, seq=seq, dec_seq=dec_seq),
        grid=(n // tl,),
        in_specs=[
            pl.BlockSpec((tl, 2 * GROUP_W), lambda i: (i, gcol)),
            pl.BlockSpec((CONV_HALO, 2 * GROUP_W), lambda i: (jnp.maximum(i * hb - 1, 0), gcol)),
            pl.BlockSpec((CONV_HALO, 2 * GROUP_W), lambda i: (jnp.minimum((i + 1) * hb, nhb - 1), gcol)),
            pl.BlockSpec((CONV_WIDTH, GROUP_W), lambda i: (0, 0)),
            pl.BlockSpec((1, GROUP_W), lambda i: (0, 0)),
            pl.BlockSpec((1, GROUP_W), lambda i: (0, 0)),
            pl.BlockSpec((1, GROUP_W), lambda i: (0, 0)),
        ],
        out_specs=pl.BlockSpec((tl, GROUP_W), lambda i: (i, 0)),
        out_shape=jax.ShapeDtypeStruct((n, GROUP_W), BF16),
        scratch_shapes=[pltpu.VMEM((tl + 2 * CONV_HALO, GROUP_W), F32),
                        pltpu.VMEM((SUBLANES - 1, tl + 2 * CONV_HALO, GROUP_W), F32)],
        compiler_params=_params("parallel"),
        name="conformer_conv",
    )(u, u, u, w, b.reshape(1, -1), lg.reshape(1, -1), lb.reshape(1, -1))


def _fnet_ch_kernel(x_ref, w_ref, o_ref):
    o_ref[...] = jnp.dot(x_ref[...], w_ref[...], preferred_element_type=F32).astype(o_ref.dtype)


def _fnet_channels(u, wch, tm):
    n = u.shape[0]
    return pl.pallas_call(
        _fnet_ch_kernel,
        grid=(n // tm,),
        in_specs=[
            pl.BlockSpec((tm, 512), lambda i: (i, COL_FOUR // 512)),
            pl.BlockSpec((512, 1024), lambda i: (0, 0)),
        ],
        out_specs=pl.BlockSpec((tm, 1024), lambda i: (i, 0)),
        out_shape=jax.ShapeDtypeStruct((n, 1024), BF16),
        compiler_params=_params("parallel"),
        name="fnet_channels",
    )(u, wch)


def _fnet_pos_kernel(wc_ref, ws_ref, z_ref, o_ref):
    o = jnp.dot(wc_ref[...], z_ref[:, :512], preferred_element_type=F32)
    o = o + jnp.dot(ws_ref[...], z_ref[:, 512:], preferred_element_type=F32)
    o_ref[...] = o.astype(o_ref.dtype)


def _fnet_positions(z, wc, ws, row0, nb, lseq):
    tm = min(512, lseq)
    r0 = row0 // lseq
    return pl.pallas_call(
        _fnet_pos_kernel,
        grid=(nb, lseq // tm),
        in_specs=[
            pl.BlockSpec((tm, lseq), lambda b, i: (i, 0)),
            pl.BlockSpec((tm, lseq), lambda b, i: (i, 0)),
            pl.BlockSpec((lseq, 1024), lambda b, i: (r0 + b, 0)),
        ],
        out_specs=pl.BlockSpec((tm, 512), lambda b, i: (b * (lseq // tm) + i, 0)),
        out_shape=jax.ShapeDtypeStruct((nb * lseq, 512), BF16),
        compiler_params=_params("parallel", "arbitrary"),
        name="fnet_positions",
    )(wc, ws, z)


def _dft_tables(n):
    rb = DFT_ROW_BLOCK if n % DFT_ROW_BLOCK == 0 else 1
    t = jnp.arange(n, dtype=jnp.int32)[None, :]

    def trig(rows):
        ang = ((rows[:, None] * t) % n).astype(F32) * (2.0 * np.pi / n)
        return jnp.cos(ang), jnp.sin(ang)

    ca, sa = trig(jnp.arange(n // rb, dtype=jnp.int32) * rb)
    cb, sb = trig(jnp.arange(rb, dtype=jnp.int32))
    scale = n ** -0.5
    cos = (ca[:, None, :] * cb[None, :, :] - sa[:, None, :] * sb[None, :, :]).reshape(n, n) * scale
    sin = (sa[:, None, :] * cb[None, :, :] + ca[:, None, :] * sb[None, :, :]).reshape(n, n) * scale
    return cos, sin


def _outproj_kernel(ys_ref, yc_ref, ac_ref, al_ref, fc_ref, fl_ref, w_ref, x_ref, mod_ref, o_ref, *, n_ctx_blocks):
    def body(att_ref, four_ref):
        acc = jnp.dot(ys_ref[...], w_ref[0, 0], preferred_element_type=F32)
        acc = acc + jnp.dot(att_ref[...], w_ref[0, 1], preferred_element_type=F32)
        acc = acc + jnp.dot(yc_ref[...], w_ref[0, 2], preferred_element_type=F32)
        acc = acc + jnp.dot(four_ref[...], w_ref[0, 3], preferred_element_type=F32)
        o_ref[...] = x_ref[...] + mod_ref[0, 0][2:3] * acc

    is_ctx = pl.program_id(0) < n_ctx_blocks
    pl.when(is_ctx)(lambda: body(ac_ref, fc_ref))
    pl.when(jnp.logical_not(is_ctx))(lambda: body(al_ref, fl_ref))


def _outproj(y_ssd, y_conv, att_c, att_l, four_c, four_l, w_all, x, mod_all, layer, mod_row, tm):
    n, d = x.shape
    tn = 1024
    nbc = att_c.shape[0] // tm
    nbl = att_l.shape[0] // tm
    full = pl.BlockSpec((tm, GROUP_W), lambda i, j: (i, 0))
    ctx = pl.BlockSpec((tm, GROUP_W), lambda i, j: (jnp.minimum(i, nbc - 1), 0))
    lat = pl.BlockSpec((tm, GROUP_W), lambda i, j: (jnp.clip(i - nbc, 0, nbl - 1), 0))
    return pl.pallas_call(
        functools.partial(_outproj_kernel, n_ctx_blocks=nbc),
        grid=(n // tm, d // tn),
        in_specs=[full, full, ctx, lat, ctx, lat,
                  pl.BlockSpec((1, 4, GROUP_W, tn), lambda i, j: (layer, 0, 0, j)),
                  pl.BlockSpec((tm, tn), lambda i, j: (i, j)),
                  pl.BlockSpec((1, 1, N_MOD, tn), lambda i, j: (layer, mod_row(i), 0, j))],
        out_specs=pl.BlockSpec((tm, tn), lambda i, j: (i, j)),
        out_shape=jax.ShapeDtypeStruct((n, d), F32),
        compiler_params=_params("parallel", "arbitrary"),
        name="outproj",
    )(y_ssd, y_conv, att_c, att_l, four_c, four_l, w_all, x, mod_all)


ROUTE_LANE0 = N_EXPERT_GROUPS


def _route(lg):
    lane = lax.broadcasted_iota(jnp.int32, lg.shape, 1)
    lanef = lane.astype(F32)
    ninf = -jnp.inf
    isg = lane < N_EXPERT_GROUPS
    mg = jnp.max(jnp.where(isg, lg, ninf), axis=-1, keepdims=True)
    gsum = jnp.sum(jnp.where(isg, jnp.exp(jnp.where(isg, lg, ninf) - mg), 0.0), axis=-1, keepdims=True)
    gwt = 1.0 / gsum
    gi = jnp.min(jnp.where(isg, jnp.where(lg == mg, lanef, 1e9), 1e9), axis=-1, keepdims=True)
    grp = jnp.where(lane < ROUTE_LANE0 + N_EXPERTS, (lane - ROUTE_LANE0) // EXPERTS_PER_GROUP, -1).astype(F32)
    el1 = jnp.where(grp == gi, lg, ninf)
    v1 = jnp.max(el1, axis=-1, keepdims=True)
    i1 = jnp.min(jnp.where(el1 == v1, lanef, 1e9), axis=-1, keepdims=True)
    el2 = jnp.where(lanef == i1, ninf, el1)
    v2 = jnp.max(el2, axis=-1, keepdims=True)
    i2 = jnp.min(jnp.where(el2 == v2, lanef, 1e9), axis=-1, keepdims=True)
    t = jnp.exp(v2 - v1)
    ew1 = 1.0 / (1.0 + t)
    ew2 = t * ew1
    return gwt * (jnp.where(lanef == i1, ew1, 0.0) + jnp.where(lanef == i2, ew2, 0.0)), gi


def _moe_kernel(x_ref, g_ref, mod_ref, wr_ref, br_ref, w1_ref, w3_ref, w2_ref, o_ref,
                hs_scr, cs_scr, outs_scr, pt_scr, seg_smem, *, ne):
    e = pl.program_id(1)
    tm = x_ref.shape[0]
    tmp = hs_scr.shape[0]
    blk = MOE_BLOCK
    lane = lax.broadcasted_iota(jnp.int32, (1, LANES), 1)

    @pl.when(e == 0)
    def _():
        x = x_ref[...]
        ms = jnp.mean(x * x, axis=-1, keepdims=True)
        y = x * lax.rsqrt(ms + EPS) * g_ref[...]
        m = mod_ref[0, 0]
        h = y * (1.0 + m[4:5]) + m[3:4]
        hi, lo = _split_bf16(h)
        lg = (jnp.dot(hi, wr_ref[0, 0], preferred_element_type=F32)
              + jnp.dot(lo, wr_ref[0, 0], preferred_element_type=F32)
              + jnp.dot(hi, wr_ref[0, 1], preferred_element_type=F32)) + br_ref[0]
        comb, gi = _route(lg)
        onehot = lane.astype(F32) == gi
        ri = lax.broadcasted_iota(jnp.int32, (tm, tm), 0)
        ci = lax.broadcasted_iota(jnp.int32, (tm, tm), 1)
        earlier = jnp.where(ri > ci, 1.0, 0.0).astype(BF16)
        before = jnp.dot(earlier, jnp.where(onehot, 1.0, 0.0).astype(BF16), preferred_element_type=F32)
        rank = jnp.sum(jnp.where(onehot, before, 0.0), axis=-1, keepdims=True)
        cnt = jnp.sum(jnp.where(onehot, 1.0, 0.0), axis=0, keepdims=True)
        padded = jnp.floor((cnt + (blk - 1)) * (1.0 / blk)) * blk
        start = sum(pltpu.roll(padded, s, 1) for s in range(1, N_EXPERT_GROUPS))
        pos = rank + jnp.sum(jnp.where(onehot, start, 0.0), axis=-1, keepdims=True)
        col = lax.broadcasted_iota(jnp.int32, (tm, tmp), 1).astype(F32)
        pt_scr[...] = jnp.where(col == pos, 1.0, 0.0).astype(BF16)
        pos_row = jnp.broadcast_to(pos, (tm, LANES)).T[0:1, :]
        row = lax.broadcasted_iota(jnp.int32, (tmp, tm), 0).astype(F32)
        perm = jnp.where(row == pos_row, 1.0, 0.0).astype(BF16)
        hs_scr[...] = jnp.dot(perm, hi, preferred_element_type=F32).astype(BF16)
        chi, clo = _split_bf16(comb)
        cs_scr[...] = (jnp.dot(perm, chi, preferred_element_type=F32)
                       + jnp.dot(perm, clo, preferred_element_type=F32))
        outs_scr[...] = jnp.zeros_like(outs_scr)
        for g in range(N_EXPERT_GROUPS):
            seg_smem[g] = jnp.sum(jnp.where(lane == g, start, 0.0)).astype(jnp.int32)
            seg_smem[N_EXPERT_GROUPS + g] = jnp.sum(jnp.where(lane == g, padded, 0.0) * (1.0 / blk)).astype(jnp.int32)

    first = e * MOE_EXPERTS_PER_STEP
    grp = first // EXPERTS_PER_GROUP
    seg_start = seg_smem[grp]

    def block(j, carry):
        r = pl.multiple_of(seg_start + j * blk, blk)
        hb = hs_scr[pl.ds(r, blk), :]
        cw = cs_scr[pl.ds(r, blk), :]
        y = None
        for k in range(MOE_EXPERTS_PER_STEP):
            a = jnp.dot(hb, w1_ref[0, k], preferred_element_type=F32)
            b = jnp.dot(hb, w3_ref[0, k], preferred_element_type=F32)
            ce = jnp.sum(jnp.where(lane == first + k + ROUTE_LANE0, cw, 0.0), axis=-1, keepdims=True)
            hid = ((a * _sigmoid(a)) * b * ce).astype(BF16)
            yk = jnp.dot(hid, w2_ref[0, k], preferred_element_type=F32)
            y = yk if y is None else y + yk
        outs_scr[pl.ds(r, blk), :] += y
        return carry

    lax.fori_loop(0, seg_smem[N_EXPERT_GROUPS + grp], block, 0)

    @pl.when(e == ne // MOE_EXPERTS_PER_STEP - 1)
    def _():
        y = jnp.dot(pt_scr[...], outs_scr[...].astype(BF16), preferred_element_type=F32)
        o_ref[...] = x_ref[...] + mod_ref[0, 0][5:6] * y


def _moe(x, g, mod_all, wr, br, w1, w3, w2, layer, mod_row, tm):
    n, d = x.shape
    ne, ff = w1.shape[1], w1.shape[3]
    tmp = tm + N_EXPERT_GROUPS * MOE_BLOCK
    return pl.pallas_call(
        functools.partial(_moe_kernel, ne=ne),
        grid=(n // tm, ne // MOE_EXPERTS_PER_STEP),
        in_specs=[
            pl.BlockSpec((tm, d), lambda i, e: (i, 0)),
            pl.BlockSpec((1, d), lambda i, e: (0, 0)),
            pl.BlockSpec((1, 1, N_MOD, d), lambda i, e: (layer, mod_row(i), 0, 0)),
            pl.BlockSpec((1, 2, d, LANES), lambda i, e: (layer, 0, 0, 0)),
            pl.BlockSpec((1, 1, LANES), lambda i, e: (layer, 0, 0)),
            pl.BlockSpec((1, MOE_EXPERTS_PER_STEP, d, ff), lambda i, e: (layer, e, 0, 0)),
            pl.BlockSpec((1, MOE_EXPERTS_PER_STEP, d, ff), lambda i, e: (layer, e, 0, 0)),
            pl.BlockSpec((1, MOE_EXPERTS_PER_STEP, ff, d), lambda i, e: (layer, e, 0, 0)),
        ],
        out_specs=pl.BlockSpec((tm, d), lambda i, e: (i, 0)),
        out_shape=jax.ShapeDtypeStruct((n, d), F32),
        scratch_shapes=[pltpu.VMEM((tmp, d), BF16), pltpu.VMEM((tmp, LANES), F32), pltpu.VMEM((tmp, d), F32),
                        pltpu.VMEM((tm, tmp), BF16), pltpu.SMEM((2 * N_EXPERT_GROUPS,), jnp.int32)],
        compiler_params=_params("parallel", "arbitrary", vmem=VMEM_LIMIT_MOE),
        name="moe",
    )(x, g.reshape(1, d), mod_all, wr, br, w1, w3, w2)


def _fnorm_kernel(x_ref, g_ref, o_ref):
    x = x_ref[...]
    ms = jnp.mean(x * x, axis=-1, keepdims=True)
    o_ref[...] = x * lax.rsqrt(ms + EPS) * g_ref[...]


def _final_norm(x, g, row0, nrows, tm):
    d = x.shape[1]
    r0 = row0 // tm
    return pl.pallas_call(
        _fnorm_kernel,
        grid=(nrows // tm,),
        in_specs=[pl.BlockSpec((tm, d), lambda i: (r0 + i, 0)), pl.BlockSpec((1, d), lambda i: (0, 0))],
        out_specs=pl.BlockSpec((tm, d), lambda i: (i, 0)),
        out_shape=jax.ShapeDtypeStruct((nrows, d), F32),
        compiler_params=_params("parallel"),
        name="final_norm",
    )(x, g.reshape(1, d))


def _rope_tables(dec_seq):
    t = jnp.arange(dec_seq, dtype=jnp.int32)
    half = ROPE_AXIS_DIM // 2
    freqs = ROPE_THETA ** (-jnp.arange(half, dtype=F32) / half)

    def axis(pos):
        ang = pos.astype(F32)[:, None] * freqs[None, :]
        c, s = jnp.cos(ang), jnp.sin(ang)
        return jnp.concatenate([c, c], axis=-1), jnp.concatenate([-s, s], axis=-1)

    cr, sr = axis(t // GRID_W)
    cc, sc = axis(t % GRID_W)
    cos = jnp.tile(jnp.concatenate([cr, cc], axis=-1), (1, ATT_HEADS))
    sin = jnp.tile(jnp.concatenate([sr, sc], axis=-1), (1, ATT_HEADS))
    return cos, sin


def _block_diag_ones(width, block):
    i = np.arange(width) // block
    return jnp.asarray((i[:, None] == i[None, :]).astype(np.float32), dtype=BF16)


def _pack_w_in(w_in):
    depth, d, _ = w_in.shape
    seg = lambda a, b: w_in[:, :, a:b]
    zeros = lambda n: jnp.zeros((depth, d, n), w_in.dtype)
    parts = [seg(SRC_XBC, SRC_DT), seg(SRC_GLU, SRC_FOUR), seg(SRC_Z, SRC_XBC), seg(SRC_Q, SRC_K),
             seg(SRC_FOUR, SRC_END), seg(SRC_K, SRC_V), seg(SRC_V, SRC_GLU),
             seg(SRC_DT, SRC_DT + SSD_HEADS), zeros(LANES - SSD_HEADS),
             seg(SRC_DT + SSD_HEADS, SRC_Q), zeros(LANES - SSD_HEADS)]
    return jnp.concatenate(parts, axis=-1).astype(BF16)


def _lanes8(v):
    return jnp.pad(v.astype(F32), ((0, 0), (0, 0), (0, LANES - SSD_HEADS)))[:, :, None, :]


def kernel(x_prompt, x_sample, cache_k, cache_v, state_ssd, c, c_ctx, norm1_g, norm2_g, w_mod, b_mod, w_in,
           ssd_conv_w, ssd_conv_b, ssd_dt_bias, ssd_A_log, ssd_D, ssd_norm_g, q_norm_g, k_norm_g, cf_dw_w, cf_dw_b,
           cf_ln_g, cf_ln_b, w_out, router_group_w, router_group_b, router_expert_w, router_expert_b, w1, w3, w2,
           final_norm_g):
    bc, lc, d = x_prompt.shape
    bl, ll, _ = x_sample.shape
    n_ctx, n_lat = bc * lc, bl * ll
    n = n_ctx + n_lat
    depth = w_in.shape[0]
    kvw = ATT_KV_HEADS * HEAD_DIM
    assert lc & (lc - 1) == 0 and ll & (ll - 1) == 0 and lc % SSD_CHUNK == 0 and ll % SSD_CHUNK == 0
    assert n_ctx % ll == 0 and ll % GRID_W == 0

    tm = _pow2_tile(1024, n_ctx, ll)
    tm_moe = _pow2_tile(512, n_ctx, ll)
    tm_qk = _pow2_tile(512, lc, ll)

    def mod_row_fn(t):
        nb, per = n_ctx // t, ll // t
        return lambda i: jnp.where(i < nb, 0, 1 + (i - nb) // per)

    w_in_p = _pack_w_in(w_in)
    w_out_p = w_out.astype(BF16).reshape(depth, 4, GROUP_W, d)
    w1_b, w3_b, w2_b = w1.astype(BF16), w3.astype(BF16), w2.astype(BF16)
    wr = jnp.concatenate([router_group_w, router_expert_w,
                          jnp.zeros((depth, d, LANES - N_EXPERT_GROUPS - N_EXPERTS), F32)], axis=-1)
    wr_hi = wr.astype(BF16)
    wr_lo = (wr - wr_hi.astype(F32)).astype(BF16)
    wr_p = jnp.stack([wr_hi, wr_lo], axis=1)
    br_p = jnp.concatenate([router_group_b, router_expert_b,
                            jnp.zeros((depth, LANES - N_EXPERT_GROUPS - N_EXPERTS), F32)], axis=-1)[:, None, :]
    dt_bias_p = _lanes8(ssd_dt_bias)
    a_log_p = _lanes8(ssd_A_log)
    d_lanes = jnp.repeat(ssd_D.astype(F32), SSD_HEADDIM, axis=-1)[:, None, :]
    expand_np = np.arange(LANES)[:, None] == (np.arange(512)[None, :] // SSD_HEADDIM)
    expand = jnp.asarray(expand_np, dtype=BF16)
    expand_t = jnp.asarray(expand_np.T, dtype=BF16)
    gq = jnp.tile(q_norm_g, (1, ATT_HEADS))[:, None, :]
    gk = jnp.tile(k_norm_g, (1, ATT_KV_HEADS))[:, None, :]
    bdq = _block_diag_ones(512, HEAD_DIM)
    bdk = _block_diag_ones(kvw, HEAD_DIM)
    rope_tabs = _rope_tables(ll)
    cch, sch = _dft_tables(FNET_GROUP_CH)
    eye4 = jnp.eye(4, dtype=F32)
    wch = jnp.concatenate([jnp.kron(eye4, cch), jnp.kron(eye4, sch)], axis=-1).astype(BF16)
    dft_c = {}
    for lseq in (lc, ll):
        cl, sl = _dft_tables(lseq)
        dft_c[lseq] = (cl.astype(BF16), (-sl).astype(BF16))
    kz_cache = _cache_variants(cache_k, ones_tail=False)
    vz_cache = _cache_variants(cache_v, ones_tail=True)

    mod_all = _adaln(jnp.concatenate([c_ctx[None, :], c], axis=0), w_mod, b_mod)

    x = jnp.concatenate([x_prompt.reshape(n_ctx, d), x_sample.reshape(n_lat, d)], axis=0)
    h0_ctx = jnp.zeros((bc, 2, 512, SSD_D_STATE), F32)
    h0_lat = state_ssd.reshape(bl, depth, 2, 512, SSD_D_STATE)
    ks, vs, ss = [], [], []
    for l in range(depth):
        u = _inproj(x, norm1_g[l], mod_all, w_in_p, l, mod_row_fn(tm), tm)

        xbc = _ssdpre(u, ssd_conv_w[l], ssd_conv_b[l], n_ctx, lc, ll)
        yf_c, yb_c, h_c = _ssd_scan(xbc, u, dt_bias_p[l], a_log_p[l], expand, expand_t, h0_ctx, 0, bc, lc)
        yf_l, yb_l, _ = _ssd_scan(xbc, u, dt_bias_p[l], a_log_p[l], expand, expand_t, h0_lat[:, l], n_ctx, bl, ll)
        y_ssd = _gnorm((yf_c, yb_c), (yf_l, yb_l), xbc, u, d_lanes[l], ssd_norm_g[l], tm)
        ss.append(h_c.reshape(bc, 2, SSD_HEADS, SSD_HEADDIM, SSD_D_STATE))

        q_c, kz_c, vz_c, k_c, v_c = _qk_prep(u, gq[l], gk[l], bdq, bdk, 0, bc, lc, tm_qk, emit_kv=True)
        q_l, kz_l, vz_l = _qk_prep(u, gq[l], gk[l], bdq, bdk, n_ctx, bl, ll, tm_qk, tables=rope_tabs)
        ks.append(k_c.reshape(bc, lc, ATT_KV_HEADS, HEAD_DIM))
        vs.append(v_c.reshape(bc, lc, ATT_KV_HEADS, HEAD_DIM))
        att_c = _attention(q_c, kz_c, vz_c, min(256, lc))
        att_l = _attention(q_l, kz_l, vz_l, min(512, ll), cache=(kz_cache, vz_cache, l))

        y_conv = _conformer(u, cf_dw_w[l], cf_dw_b[l], cf_ln_g[l], cf_ln_b[l], n_ctx, lc, ll)

        zf = _fnet_channels(u, wch, tm)
        four_c = _fnet_positions(zf, *dft_c[lc], 0, bc, lc)
        four_l = _fnet_positions(zf, *dft_c[ll], n_ctx, bl, ll)

        x = _outproj(y_ssd, y_conv, att_c, att_l, four_c, four_l, w_out_p, x, mod_all, l, mod_row_fn(tm), tm)
        x = _moe(x, norm2_g[l], mod_all, wr_p, br_p, w1_b, w3_b, w2_b, l, mod_row_fn(tm_moe), tm_moe)

    y_prompt = _final_norm(x, final_norm_g, 0, n_ctx, tm).reshape(bc, lc, d)
    y_sample = _final_norm(x, final_norm_g, n_ctx, n_lat, tm).reshape(bl, ll, d)
    return (y_prompt, y_sample, jnp.stack(ks, axis=1), jnp.stack(vs, axis=1), jnp.stack(ss, axis=1))
```

```python
import functools

import numpy as np
import jax
import jax.numpy as jnp
from jax import lax
from jax.experimental import pallas as pl
from jax.experimental.pallas import tpu as pltpu

F32 = jnp.float32
BF16 = jnp.bfloat16
HIGHEST = lax.Precision.HIGHEST

D_MODEL = 2048
DEPTH = 4
GRID_W = 64
GROUP_W = 512
SSD_HEADS = 8
SSD_HEADDIM = 64
SSD_GROUPS = 2
SSD_D_STATE = 128
SSD_CHUNK = 128
HEAD_DIM = 64
ATT_HEADS = 8
ATT_KV_HEADS = 2
ROPE_THETA = 10000.0
ROPE_AXIS_DIM = 32
CONV_WIDTH = 31
CONV_HALO = 16
SUBLANES = 8
FNET_GROUP_CH = 128
DFT_ROW_BLOCK = 64
N_EXPERT_GROUPS = 4
EXPERTS_PER_GROUP = 4
N_EXPERTS = 16
EXPERT_FF = 256
N_MOD = 6
EPS = 1e-6
LANES = 128

U_WIDTH = 4096
COL_XBC, COL_GLU, COL_Z, COL_Q, COL_FOUR, COL_K, COL_V, COL_DTF, COL_DTB = (
    0, 1024, 2048, 2560, 3072, 3584, 3712, 3840, 3968)
SRC_Z, SRC_XBC, SRC_DT, SRC_Q, SRC_K, SRC_V, SRC_GLU, SRC_FOUR, SRC_END = (
    0, 512, 1536, 1552, 2064, 2192, 2320, 3344, 3856)

VMEM_LIMIT = 48 * 1024 * 1024
VMEM_LIMIT_MOE = 54 * 1024 * 1024

NT_DIMS = (((1,), (1,)), ((), ()))
Q_SCALE = HEAD_DIM ** -0.5 * float(np.log2(np.e))
ATT_KEY_CHUNK = 1024
MOE_BLOCK = 128
MOE_EXPERTS_PER_STEP = 2
assert EXPERTS_PER_GROUP % MOE_EXPERTS_PER_STEP == 0
SSD_SEQ_BLOCK = 2


def _params(*sem, vmem=VMEM_LIMIT):
    return pltpu.CompilerParams(dimension_semantics=sem, vmem_limit_bytes=vmem)


def _sigmoid(x):
    return 1.0 / (1.0 + jnp.exp(-x))


def _split_bf16(x):
    hi = x.astype(BF16)
    lo = (x - hi.astype(F32)).astype(BF16)
    return hi, lo


def _pow2_tile(pref, *dims):
    t = pref
    while any(d % t for d in dims):
        t //= 2
    return t


def _mod_kernel(c_ref, w_ref, b_ref, o_ref, acc_ref, *, nk):
    k = pl.program_id(2)

    @pl.when(k == 0)
    def _():
        acc_ref[...] = jnp.zeros_like(acc_ref)

    w = w_ref[0]
    tk, tn = w.shape
    for r in range(c_ref.shape[0]):
        c = c_ref[r]
        s = c * _sigmoid(c)
        acc_ref[r] += (s * w).reshape(tk // SUBLANES, SUBLANES, tn).sum(axis=0)

    @pl.when(k == nk - 1)
    def _():
        o_ref[0] = acc_ref[...].sum(axis=1) + b_ref[0]


def _adaln(cvec, w_mod, b_mod):
    r, d = cvec.shape
    depth, _, n = w_mod.shape
    tk, tn = 1024, 2048
    nk = d // tk
    out = pl.pallas_call(
        functools.partial(_mod_kernel, nk=nk),
        grid=(depth, n // tn, nk),
        in_specs=[
            pl.BlockSpec((r, tk, 1), lambda l, j, k: (0, k, 0)),
            pl.BlockSpec((1, tk, tn), lambda l, j, k: (l, k, j)),
            pl.BlockSpec((1, 1, tn), lambda l, j, k: (l, 0, j)),
        ],
        out_specs=pl.BlockSpec((1, r, tn), lambda l, j, k: (l, 0, j)),
        out_shape=jax.ShapeDtypeStruct((depth, r, n), F32),
        scratch_shapes=[pltpu.VMEM((r, SUBLANES, tn), F32)],
        compiler_params=_params("parallel", "parallel", "arbitrary"),
        name="adaln",
    )(cvec.reshape(r, d, 1), w_mod, b_mod.reshape(depth, 1, n))
    return out.reshape(depth, r, N_MOD, d)


def _inproj_kernel(x_ref, g_ref, mod_ref, w_ref, o_ref):
    x = x_ref[...]
    ms = jnp.mean(x * x, axis=-1, keepdims=True)
    y = x * lax.rsqrt(ms + EPS) * g_ref[...]
    m = mod_ref[0, 0]
    h = (y * (1.0 + m[1:2]) + m[0:1]).astype(BF16)
    o_ref[...] = jnp.dot(h, w_ref[0], preferred_element_type=F32).astype(o_ref.dtype)


def _inproj(x, g, mod_all, w_all, layer, mod_row, tm):
    n, d = x.shape
    nout = w_all.shape[2]
    tn = 2048
    return pl.pallas_call(
        _inproj_kernel,
        grid=(n // tm, nout // tn),
        in_specs=[
            pl.BlockSpec((tm, d), lambda i, j: (i, 0)),
            pl.BlockSpec((1, d), lambda i, j: (0, 0)),
            pl.BlockSpec((1, 1, N_MOD, d), lambda i, j: (layer, mod_row(i), 0, 0)),
            pl.BlockSpec((1, d, tn), lambda i, j: (layer, 0, j)),
        ],
        out_specs=pl.BlockSpec((tm, tn), lambda i, j: (i, j)),
        out_shape=jax.ShapeDtypeStruct((n, nout), BF16),
        compiler_params=_params("parallel", "arbitrary"),
        name="inproj",
    )(x, g.reshape(1, d), mod_all, w_all)


def _ssdpre_kernel(x_ref, w_ref, b_ref, o_ref, *, n_ctx_blocks, seq, dec_seq):
    x = x_ref[...].astype(F32)
    tl = x.shape[0]
    lseq = jnp.where(pl.program_id(0) < n_ctx_blocks, seq, dec_seq)
    pos = lax.broadcasted_iota(jnp.int32, (tl, 1), 0) & (lseq - 1)
    xp = jnp.where(pos == 0, 0.0, pltpu.roll(x, 1, 0))
    xn = jnp.where(pos == lseq - 1, 0.0, pltpu.roll(x, tl - 1, 0))
    w = w_ref[...]
    y = xp * w[0:1] + x * w[1:2] + xn * w[2:3] + b_ref[...]
    o_ref[...] = (y * _sigmoid(y)).astype(o_ref.dtype)


def _ssdpre(u, w, b, n_ctx, seq, dec_seq):
    n = u.shape[0]
    c = w.shape[1]
    tl = max(seq, dec_seq)
    assert n_ctx % tl == 0 and tl % seq == 0 and tl % dec_seq == 0
    tc = 256
    return pl.pallas_call(
        functools.partial(_ssdpre_kernel, n_ctx_blocks=n_ctx // tl, seq=seq, dec_seq=dec_seq),
        grid=(n // tl, c // tc),
        in_specs=[
            pl.BlockSpec((tl, tc), lambda i, j: (i, j + COL_XBC // tc)),
            pl.BlockSpec((3, tc), lambda i, j: (0, j)),
            pl.BlockSpec((1, tc), lambda i, j: (0, j)),
        ],
        out_specs=pl.BlockSpec((tl, tc), lambda i, j: (i, j)),
        out_shape=jax.ShapeDtypeStruct((n, c), BF16),
        compiler_params=_params("parallel", "parallel"),
        name="ssd_conv",
    )(u, w, b.reshape(1, c))


def _ssd_direction(d, x_ref, bc_ref, dt_ref, bias, alog, e, et, h_scr, y_ref):
    q = x_ref.shape[0]
    hpg = SSD_HEADS // SSD_GROUPS
    gw = hpg * SSD_HEADDIM
    x = x_ref[...].astype(F32)
    bc = bc_ref[...]
    raw = dt_ref[...].astype(F32) + bias
    dt = jnp.maximum(raw, 0.0) + jnp.log1p(jnp.exp(-jnp.abs(raw)))
    a = dt * (-jnp.exp(alog))
    ri = lax.broadcasted_iota(jnp.int32, (q, q), 0)
    ci = lax.broadcasted_iota(jnp.int32, (q, q), 1)
    tri = (ri >= ci).astype(F32)
    cs = jnp.dot(tri, a, precision=HIGHEST, preferred_element_type=F32)
    total = cs[q - 1:q, :]
    csa = cs if d == 0 else total - cs + a
    mask = (ri >= ci) if d == 0 else (ri <= ci)
    lhs = jnp.concatenate([dt * jnp.exp(total - csa), jnp.exp(csa)], axis=0)
    hi, lo = _split_bf16(lhs)
    ex = jnp.dot(hi, e, preferred_element_type=F32) + jnp.dot(lo, e, preferred_element_type=F32)
    w_x = ex[0:q]
    ecs_x = ex[q:2 * q]
    cd_t = jnp.exp(jnp.broadcast_to(total, (LANES, LANES)).T)
    hi, lo = _split_bf16(cd_t)
    cd_rows = jnp.dot(et, hi, preferred_element_type=F32) + jnp.dot(et, lo, preferred_element_type=F32)
    csa_t = csa.T
    dt_t = dt.T
    xb = x_ref[...]
    xw = x * w_x
    lane_head = lax.broadcasted_iota(jnp.int32, (1, gw), 1) // SSD_HEADDIM
    for g in range(SSD_GROUPS):
        bm = bc[:, g * SSD_D_STATE:(g + 1) * SSD_D_STATE]
        cm = bc[:, (SSD_GROUPS + g) * SSD_D_STATE:(SSD_GROUPS + g + 1) * SSD_D_STATE]
        cb = lax.dot_general(cm, bm, NT_DIMS, preferred_element_type=F32)
        xg = xb[:, g * gw:(g + 1) * gw]
        h_g = h_scr[g * gw:(g + 1) * gw, :]
        y_g = lax.dot_general(cm, h_g.astype(BF16), NT_DIMS, preferred_element_type=F32)
        y_g = y_g * ecs_x[:, g * gw:(g + 1) * gw]
        for hh in range(hpg):
            h = g * hpg + hh
            seg = csa[:, h:h + 1] - csa_t[h:h + 1, :]
            decay = jnp.exp(jnp.where(mask, seg, -jnp.inf))
            sc = (cb * decay * dt_t[h:h + 1, :]).astype(BF16)
            xm = jnp.where(lane_head == hh, xg, jnp.zeros_like(xg))
            y_g = y_g + jnp.dot(sc, xm, preferred_element_type=F32)
        xw_t = xw[:, g * gw:(g + 1) * gw].T.astype(BF16)
        st = jnp.dot(xw_t, bm, preferred_element_type=F32)
        y_ref[:, g * gw:(g + 1) * gw] = y_g.astype(y_ref.dtype)
        h_scr[g * gw:(g + 1) * gw, :] = cd_rows[g * gw:(g + 1) * gw, :] * h_g + st


def _ssd_kernel(xf_ref, bcf_ref, dtf_ref, xb_ref, bcb_ref, dtb_ref, bias_ref, alog_ref, e_ref, et_ref, h0_ref,
                yf_ref, yb_ref, hout_ref, h_scr, *, nc):
    c = pl.program_id(1)

    @pl.when(c == 0)
    def _():
        h_scr[...] = h0_ref[...]

    e = e_ref[...]
    et = et_ref[...]
    for s in range(xf_ref.shape[0]):
        _ssd_direction(0, xf_ref.at[s], bcf_ref.at[s], dtf_ref.at[s], bias_ref[0], alog_ref[0], e, et,
                       h_scr.at[s, 0], yf_ref.at[s])
        _ssd_direction(1, xb_ref.at[s], bcb_ref.at[s], dtb_ref.at[s], bias_ref[1], alog_ref[1], e, et,
                       h_scr.at[s, 1], yb_ref.at[s])

    @pl.when(c == nc - 1)
    def _():
        hout_ref[...] = h_scr[...]


def _ssd_scan(xbc, u, bias, alog, expand, expand_t, h0, row0, nb, lseq):
    q = SSD_CHUNK
    nc = lseq // q
    n = xbc.shape[0]
    s0 = row0 // lseq
    sb = 2 * SSD_SEQ_BLOCK if nc <= 2 else SSD_SEQ_BLOCK
    if nb % sb or s0 % sb:
        sb = 1
    xbc3 = xbc.reshape(n // lseq, lseq, xbc.shape[1])
    u3 = u.reshape(n // lseq, lseq, u.shape[1])
    sq = lambda b: s0 // sb + b
    sspec = pl.BlockSpec((sb, 2, 512, SSD_D_STATE), lambda b, c: (b, 0, 0, 0))
    yf, yb, hout = pl.pallas_call(
        functools.partial(_ssd_kernel, nc=nc),
        grid=(nb // sb, nc),
        in_specs=[
            pl.BlockSpec((sb, q, 512), lambda b, c: (sq(b), c, 0)),
            pl.BlockSpec((sb, q, 512), lambda b, c: (sq(b), c, 1)),
            pl.BlockSpec((sb, q, LANES), lambda b, c: (sq(b), c, COL_DTF // LANES)),
            pl.BlockSpec((sb, q, 512), lambda b, c: (sq(b), nc - 1 - c, 0)),
            pl.BlockSpec((sb, q, 512), lambda b, c: (sq(b), nc - 1 - c, 1)),
            pl.BlockSpec((sb, q, LANES), lambda b, c: (sq(b), nc - 1 - c, COL_DTB // LANES)),
            pl.BlockSpec((2, 1, LANES), lambda b, c: (0, 0, 0)),
            pl.BlockSpec((2, 1, LANES), lambda b, c: (0, 0, 0)),
            pl.BlockSpec((LANES, 512), lambda b, c: (0, 0)),
            pl.BlockSpec((512, LANES), lambda b, c: (0, 0)),
            sspec,
        ],
        out_specs=[
            pl.BlockSpec((sb, q, 512), lambda b, c: (b, c, 0)),
            pl.BlockSpec((sb, q, 512), lambda b, c: (b, nc - 1 - c, 0)),
            sspec,
        ],
        out_shape=[
            jax.ShapeDtypeStruct((nb, lseq, 512), BF16),
            jax.ShapeDtypeStruct((nb, lseq, 512), BF16),
            jax.ShapeDtypeStruct((nb, 2, 512, SSD_D_STATE), F32),
        ],
        scratch_shapes=[pltpu.VMEM((sb, 2, 512, SSD_D_STATE), F32)],
        compiler_params=_params("parallel", "arbitrary"),
        name="ssd_scan",
    )(xbc3, xbc3, u3, xbc3, xbc3, u3, bias, alog, expand, expand_t, h0)
    return yf.reshape(nb * lseq, 512), yb.reshape(nb * lseq, 512), hout


def _gnorm_kernel(yfc_ref, ybc_ref, yfl_ref, ybl_ref, x_ref, z_ref, dl_ref, g_ref, o_ref, *, n_ctx_blocks):
    def body(yf_ref, yb_ref):
        y = yf_ref[...].astype(F32) + yb_ref[...].astype(F32) + x_ref[...].astype(F32) * dl_ref[...]
        z = z_ref[...].astype(F32)
        t = y * (z * _sigmoid(z))
        ms = jnp.mean(t * t, axis=-1, keepdims=True)
        o_ref[...] = (t * lax.rsqrt(ms + EPS) * g_ref[...]).astype(o_ref.dtype)

    is_ctx = pl.program_id(0) < n_ctx_blocks
    pl.when(is_ctx)(lambda: body(yfc_ref, ybc_ref))
    pl.when(jnp.logical_not(is_ctx))(lambda: body(yfl_ref, ybl_ref))


def _gnorm(y_c, y_l, xbc, u, d_lanes, g, tm):
    n = xbc.shape[0]
    nbc = y_c[0].shape[0] // tm
    nbl = y_l[0].shape[0] // tm
    ctx = pl.BlockSpec((tm, 512), lambda i: (jnp.minimum(i, nbc - 1), 0))
    lat = pl.BlockSpec((tm, 512), lambda i: (jnp.clip(i - nbc, 0, nbl - 1), 0))
    return pl.pallas_call(
        functools.partial(_gnorm_kernel, n_ctx_blocks=nbc),
        grid=(n // tm,),
        in_specs=[
            ctx, ctx, lat, lat,
            pl.BlockSpec((tm, 512), lambda i: (i, 0)),
            pl.BlockSpec((tm, 512), lambda i: (i, COL_Z // 512)),
            pl.BlockSpec((1, 512), lambda i: (0, 0)),
            pl.BlockSpec((1, 512), lambda i: (0, 0)),
        ],
        out_specs=pl.BlockSpec((tm, 512), lambda i: (i, 0)),
        out_shape=jax.ShapeDtypeStruct((n, 512), BF16),
        compiler_params=_params("parallel"),
        name="ssd_gate_norm",
    )(*y_c, *y_l, xbc, u, d_lanes, g.reshape(1, 512))


def _head_rms(t, bd, gain):
    hi, lo = _split_bf16(t * t)
    ss = jnp.dot(hi, bd, preferred_element_type=F32) + jnp.dot(lo, bd, preferred_element_type=F32)
    return t * lax.rsqrt(ss * (1.0 / HEAD_DIM) + EPS) * gain


def _rope(t, cos, sin):
    w = t.shape[1]
    lane = lax.broadcasted_iota(jnp.int32, (1, w), 1)
    first = (lane & (ROPE_AXIS_DIM - 1)) < ROPE_AXIS_DIM // 2
    partner = jnp.where(first, pltpu.roll(t, w - ROPE_AXIS_DIM // 2, 1), pltpu.roll(t, ROPE_AXIS_DIM // 2, 1))
    return t * cos + partner * sin


def _dup_variants(t, o_ref, ones_tail):
    w = 2 * HEAD_DIM
    lane = lax.broadcasted_iota(jnp.int32, (1, w), 1)
    lo_half = lane < HEAD_DIM
    sw = pltpu.roll(t, HEAD_DIM, 1)
    o_ref[0, 0, :, 0:w] = jnp.where(lo_half, t, sw).astype(o_ref.dtype)
    o_ref[0, 1, :, 0:w] = jnp.where(lo_half, sw, t).astype(o_ref.dtype)
    if ones_tail:
        ones = jnp.ones(t.shape, o_ref.dtype)
        o_ref[0, 0, :, w:2 * w] = ones
        o_ref[0, 1, :, w:2 * w] = ones


def _qk_kernel(*refs, rope, emit_kv):
    q_ref, k_ref, v_ref, gq_ref, gk_ref, bdq_ref, bdk_ref = refs[:7]
    refs = refs[7:]
    if rope:
        cos_ref, sin_ref, cosk_ref, sink_ref = refs[:4]
        refs = refs[4:]
    qo_ref, kz_ref, vz_ref = refs[:3]
    qn = _head_rms(q_ref[...].astype(F32), bdq_ref[...], gq_ref[...])
    kn = _head_rms(k_ref[...].astype(F32), bdk_ref[...], gk_ref[...])
    v = v_ref[...].astype(F32)
    if emit_kv:
        ko_ref, vo_ref = refs[3:5]
        ko_ref[...] = kn
        vo_ref[...] = v
    if rope:
        qn = _rope(qn, cos_ref[...], sin_ref[...])
        kn = _rope(kn, cosk_ref[...], sink_ref[...])
    qo_ref[...] = (qn * Q_SCALE).astype(qo_ref.dtype)
    _dup_variants(kn, kz_ref, ones_tail=False)
    _dup_variants(v, vz_ref, ones_tail=True)


def _qk_prep(u, gq, gk, bdq, bdk, row0, nb, lseq, tm, tables=None, emit_kv=False):
    kw = ATT_KV_HEADS * HEAD_DIM
    nrows = nb * lseq
    r0 = row0 // tm
    per = lseq // tm
    in_specs = [
        pl.BlockSpec((tm, 512), lambda i: (r0 + i, COL_Q // 512)),
        pl.BlockSpec((tm, kw), lambda i: (r0 + i, COL_K // kw)),
        pl.BlockSpec((tm, kw), lambda i: (r0 + i, COL_V // kw)),
        pl.BlockSpec((1, 512), lambda i: (0, 0)),
        pl.BlockSpec((1, kw), lambda i: (0, 0)),
        pl.BlockSpec((512, 512), lambda i: (0, 0)),
        pl.BlockSpec((kw, kw), lambda i: (0, 0)),
    ]
    args = [u, u, u, gq, gk, bdq, bdk]
    if tables is not None:
        cos, sin = tables
        in_specs += [
            pl.BlockSpec((tm, 512), lambda i: (i % per, 0)),
            pl.BlockSpec((tm, 512), lambda i: (i % per, 0)),
            pl.BlockSpec((tm, kw), lambda i: (i % per, 0)),
            pl.BlockSpec((tm, kw), lambda i: (i % per, 0)),
        ]
        args += [cos, sin, cos, sin]
    zspec = pl.BlockSpec((1, ATT_KV_HEADS, tm, kw), lambda i: (i // per, 0, i % per, 0))
    out_specs = [pl.BlockSpec((tm, 512), lambda i: (i, 0)), zspec, zspec]
    vspec = pl.BlockSpec((1, ATT_KV_HEADS, tm, 2 * kw), lambda i: (i // per, 0, i % per, 0))
    out_specs[2] = vspec
    out_shape = [jax.ShapeDtypeStruct((nrows, 512), BF16),
                 jax.ShapeDtypeStruct((nb, ATT_KV_HEADS, lseq, kw), BF16),
                 jax.ShapeDtypeStruct((nb, ATT_KV_HEADS, lseq, 2 * kw), BF16)]
    if emit_kv:
        out_specs += [pl.BlockSpec((tm, kw), lambda i: (i, 0))] * 2
        out_shape += [jax.ShapeDtypeStruct((nrows, kw), F32)] * 2
    return pl.pallas_call(
        functools.partial(_qk_kernel, rope=tables is not None, emit_kv=emit_kv),
        grid=(nrows // tm,),
        in_specs=in_specs,
        out_specs=out_specs,
        out_shape=out_shape,
        compiler_params=_params("parallel"),
        name="qk_norm_rope" if tables is not None else "qk_norm",
    )(*args)


def _attn_kernel(*refs, cached):
    if cached:
        q_ref, kn_ref, vn_ref, kc_ref, vc_ref, o_ref = refs
    else:
        q_ref, kn_ref, vn_ref, o_ref = refs
    pair_w = 2 * HEAD_DIM
    tq = q_ref.shape[0]
    ck = min(ATT_KEY_CHUNK, kn_ref.shape[2])
    lo_half = lax.broadcasted_iota(jnp.int32, (1, pair_w), 1) < HEAD_DIM
    for pair in range(ATT_HEADS // 2):
        kv = pair // (ATT_HEADS // ATT_KV_HEADS // 2)
        qp = q_ref[:, pair * pair_w:(pair + 1) * pair_w]
        zero = jnp.zeros_like(qp)
        q2 = jnp.concatenate([jnp.where(lo_half, qp, zero), jnp.where(lo_half, zero, qp)], axis=0)
        chunks = [(kn_ref.at[0, kv], vn_ref.at[0, kv], j * ck, ck) for j in range(kn_ref.shape[2] // ck)]
        if cached:
            chunks.append((kc_ref.at[0, 0, kv], vc_ref.at[0, 0, kv], 0, kc_ref.shape[3]))
        parts = []
        for k_ref, v_ref, start, size in chunks:
            s = lax.dot_general(q2, k_ref[start:start + size, :], NT_DIMS, preferred_element_type=F32)
            m_c = jnp.max(s, axis=-1, keepdims=True)
            o_c = jnp.dot(jnp.exp2(s - m_c).astype(BF16), v_ref[start:start + size, :], preferred_element_type=F32)
            parts.append((m_c, o_c))
        m = functools.reduce(jnp.maximum, [m_c for m_c, _ in parts])
        o = sum(o_c * jnp.exp2(m_c - m) for m_c, o_c in parts)
        o = o[:, :pair_w] / o[:, pair_w:]
        o_ref[:, pair * pair_w:(pair + 1) * pair_w] = jnp.where(lo_half, o[:tq], o[tq:]).astype(o_ref.dtype)


def _attention(q, kz, vz, tq, cache=None):
    nb, nkv, l, kw = kz.shape
    per = l // tq
    in_specs = [
        pl.BlockSpec((tq, 512), lambda b, i: (b * per + i, 0)),
        pl.BlockSpec((1, nkv, l, kw), lambda b, i: (b, 0, 0, 0)),
        pl.BlockSpec((1, nkv, l, 2 * kw), lambda b, i: (b, 0, 0, 0)),
    ]
    args = [q, kz, vz]
    if cache is not None:
        kzc, vzc, layer = cache
        past = kzc.shape[3]
        in_specs += [pl.BlockSpec((1, 1, nkv, past, kw), lambda b, i: (b, layer, 0, 0, 0)),
                     pl.BlockSpec((1, 1, nkv, past, 2 * kw), lambda b, i: (b, layer, 0, 0, 0))]
        args += [kzc, vzc]
    return pl.pallas_call(
        functools.partial(_attn_kernel, cached=cache is not None),
        grid=(nb, per),
        in_specs=in_specs,
        out_specs=pl.BlockSpec((tq, 512), lambda b, i: (b * per + i, 0)),
        out_shape=jax.ShapeDtypeStruct((nb * l, 512), BF16),
        compiler_params=_params("parallel", "parallel"),
        name="attention_cached" if cache is not None else "attention",
    )(*args)


def _cache_variants(t, ones_tail):
    tb = t.astype(BF16).transpose(0, 1, 3, 2, 4)
    parts = [tb, tb] + ([jnp.ones(tb.shape[:-1] + (2 * HEAD_DIM,), BF16)] if ones_tail else [])
    return jnp.concatenate(parts, axis=-1)


def _glu(t):
    t = t.astype(F32)
    return t[:, :GROUP_W] * _sigmoid(t[:, GROUP_W:])


def _conf_kernel(m_ref, p_ref, n_ref, w_ref, b_ref, lg_ref, lb_ref, o_ref, scr, sh_scr, *, n_ctx_blocks, seq,
                 dec_seq):
    i = pl.program_id(0)
    tl = m_ref.shape[0]
    rows = tl + 2 * CONV_HALO
    lseq = jnp.where(i < n_ctx_blocks, seq, dec_seq)
    has_prev = ((i * tl) & (lseq - 1)) != 0
    has_next = (((i + 1) * tl) & (lseq - 1)) != 0
    scr[0:CONV_HALO, :] = jnp.where(has_prev, _glu(p_ref[...]), 0.0)
    scr[CONV_HALO:CONV_HALO + tl, :] = _glu(m_ref[...])
    scr[CONV_HALO + tl:rows, :] = jnp.where(has_next, _glu(n_ref[...]), 0.0)
    full = scr[...]
    for b in range(1, SUBLANES):
        sh_scr[b - 1] = pltpu.roll(full, rows - b, 0)
    acc = jnp.zeros((tl, GROUP_W), F32) + b_ref[...]
    off = CONV_HALO - CONV_WIDTH // 2
    for k in range(CONV_WIDTH):
        a, b = divmod(off + k, SUBLANES)
        src = scr if b == 0 else sh_scr.at[b - 1]
        acc = acc + src[a * SUBLANES:a * SUBLANES + tl, :] * w_ref[k:k + 1, :]
    mu = jnp.mean(acc, axis=-1, keepdims=True)
    xc = acc - mu
    var = jnp.mean(xc * xc, axis=-1, keepdims=True)
    y = xc * lax.rsqrt(var + EPS) * lg_ref[...] + lb_ref[...]
    o_ref[...] = (y * _sigmoid(y)).astype(o_ref.dtype)


def _conformer(u, w, b, lg, lb, n_ctx, seq, dec_seq):
    n = u.shape[0]
    tl = _pow2_tile(256, seq, dec_seq)
    hb = tl // CONV_HALO
    nhb = n // CONV_HALO
    gcol = COL_GLU // (2 * GROUP_W)
    return pl.pallas_call(
        functools.partial(_conf_kernel, n_ctx_blocks=n_ctx // tl, seq=seq, dec_seq=dec_seq),
        grid=(n // tl,),
        in_specs=[
            pl.BlockSpec((tl, 2 * GROUP_W), lambda i: (i, gcol)),
            pl.BlockSpec((CONV_HALO, 2 * GROUP_W), lambda i: (jnp.maximum(i * hb - 1, 0), gcol)),
            pl.BlockSpec((CONV_HALO, 2 * GROUP_W), lambda i: (jnp.minimum((i + 1) * hb, nhb - 1), gcol)),
            pl.BlockSpec((CONV_WIDTH, GROUP_W), lambda i: (0, 0)),
            pl.BlockSpec((1, GROUP_W), lambda i: (0, 0)),
            pl.BlockSpec((1, GROUP_W), lambda i: (0, 0)),
            pl.BlockSpec((1, GROUP_W), lambda i: (0, 0)),
        ],
        out_specs=pl.BlockSpec((tl, GROUP_W), lambda i: (i, 0)),
        out_shape=jax.ShapeDtypeStruct((n, GROUP_W), BF16),
        scratch_shapes=[pltpu.VMEM((tl + 2 * CONV_HALO, GROUP_W), F32),
                        pltpu.VMEM((SUBLANES - 1, tl + 2 * CONV_HALO, GROUP_W), F32)],
        compiler_params=_params("parallel"),
        name="conformer_conv",
    )(u, u, u, w, b.reshape(1, -1), lg.reshape(1, -1), lb.reshape(1, -1))


def _fnet_ch_kernel(x_ref, w_ref, o_ref):
    o_ref[...] = jnp.dot(x_ref[...], w_ref[...], preferred_element_type=F32).astype(o_ref.dtype)


def _fnet_channels(u, wch, tm):
    n = u.shape[0]
    return pl.pallas_call(
        _fnet_ch_kernel,
        grid=(n // tm,),
        in_specs=[
            pl.BlockSpec((tm, 512), lambda i: (i, COL_FOUR // 512)),
            pl.BlockSpec((512, 1024), lambda i: (0, 0)),
        ],
        out_specs=pl.BlockSpec((tm, 1024), lambda i: (i, 0)),
        out_shape=jax.ShapeDtypeStruct((n, 1024), BF16),
        compiler_params=_params("parallel"),
        name="fnet_channels",
    )(u, wch)


def _fnet_pos_kernel(wc_ref, ws_ref, z_ref, o_ref):
    o = jnp.dot(wc_ref[...], z_ref[:, :512], preferred_element_type=F32)
    o = o + jnp.dot(ws_ref[...], z_ref[:, 512:], preferred_element_type=F32)
    o_ref[...] = o.astype(o_ref.dtype)


def _fnet_positions(z, wc, ws, row0, nb, lseq):
    tm = min(512, lseq)
    r0 = row0 // lseq
    return pl.pallas_call(
        _fnet_pos_kernel,
        grid=(nb, lseq // tm),
        in_specs=[
            pl.BlockSpec((tm, lseq), lambda b, i: (i, 0)),
            pl.BlockSpec((tm, lseq), lambda b, i: (i, 0)),
            pl.BlockSpec((lseq, 1024), lambda b, i: (r0 + b, 0)),
        ],
        out_specs=pl.BlockSpec((tm, 512), lambda b, i: (b * (lseq // tm) + i, 0)),
        out_shape=jax.ShapeDtypeStruct((nb * lseq, 512), BF16),
        compiler_params=_params("parallel", "arbitrary"),
        name="fnet_positions",
    )(wc, ws, z)


def _dft_tables(n):
    rb = DFT_ROW_BLOCK if n % DFT_ROW_BLOCK == 0 else 1
    t = jnp.arange(n, dtype=jnp.int32)[None, :]

    def trig(rows):
        ang = ((rows[:, None] * t) % n).astype(F32) * (2.0 * np.pi / n)
        return jnp.cos(ang), jnp.sin(ang)

    ca, sa = trig(jnp.arange(n // rb, dtype=jnp.int32) * rb)
    cb, sb = trig(jnp.arange(rb, dtype=jnp.int32))
    scale = n ** -0.5
    cos = (ca[:, None, :] * cb[None, :, :] - sa[:, None, :] * sb[None, :, :]).reshape(n, n) * scale
    sin = (sa[:, None, :] * cb[None, :, :] + ca[:, None, :] * sb[None, :, :]).reshape(n, n) * scale
    return cos, sin


def _outproj_kernel(ys_ref, yc_ref, ac_ref, al_ref, fc_ref, fl_ref, w_ref, x_ref, mod_ref, o_ref, *, n_ctx_blocks):
    def body(att_ref, four_ref):
        acc = jnp.dot(ys_ref[...], w_ref[0, 0], preferred_element_type=F32)
        acc = acc + jnp.dot(att_ref[...], w_ref[0, 1], preferred_element_type=F32)
        acc = acc + jnp.dot(yc_ref[...], w_ref[0, 2], preferred_element_type=F32)
        acc = acc + jnp.dot(four_ref[...], w_ref[0, 3], preferred_element_type=F32)
        o_ref[...] = x_ref[...] + mod_ref[0, 0][2:3] * acc

    is_ctx = pl.program_id(0) < n_ctx_blocks
    pl.when(is_ctx)(lambda: body(ac_ref, fc_ref))
    pl.when(jnp.logical_not(is_ctx))(lambda: body(al_ref, fl_ref))


def _outproj(y_ssd, y_conv, att_c, att_l, four_c, four_l, w_all, x, mod_all, layer, mod_row, tm):
    n, d = x.shape
    tn = 1024
    nbc = att_c.shape[0] // tm
    nbl = att_l.shape[0] // tm
    full = pl.BlockSpec((tm, GROUP_W), lambda i, j: (i, 0))
    ctx = pl.BlockSpec((tm, GROUP_W), lambda i, j: (jnp.minimum(i, nbc - 1), 0))
    lat = pl.BlockSpec((tm, GROUP_W), lambda i, j: (jnp.clip(i - nbc, 0, nbl - 1), 0))
    return pl.pallas_call(
        functools.partial(_outproj_kernel, n_ctx_blocks=nbc),
        grid=(n // tm, d // tn),
        in_specs=[full, full, ctx, lat, ctx, lat,
                  pl.BlockSpec((1, 4, GROUP_W, tn), lambda i, j: (layer, 0, 0, j)),
                  pl.BlockSpec((tm, tn), lambda i, j: (i, j)),
                  pl.BlockSpec((1, 1, N_MOD, tn), lambda i, j: (layer, mod_row(i), 0, j))],
        out_specs=pl.BlockSpec((tm, tn), lambda i, j: (i, j)),
        out_shape=jax.ShapeDtypeStruct((n, d), F32),
        compiler_params=_params("parallel", "arbitrary"),
        name="outproj",
    )(y_ssd, y_conv, att_c, att_l, four_c, four_l, w_all, x, mod_all)


ROUTE_LANE0 = N_EXPERT_GROUPS


def _route(lg):
    lane = lax.broadcasted_iota(jnp.int32, lg.shape, 1)
    lanef = lane.astype(F32)
    ninf = -jnp.inf
    isg = lane < N_EXPERT_GROUPS
    mg = jnp.max(jnp.where(isg, lg, ninf), axis=-1, keepdims=True)
    gsum = jnp.sum(jnp.where(isg, jnp.exp(jnp.where(isg, lg, ninf) - mg), 0.0), axis=-1, keepdims=True)
    gwt = 1.0 / gsum
    gi = jnp.min(jnp.where(isg, jnp.where(lg == mg, lanef, 1e9), 1e9), axis=-1, keepdims=True)
    grp = jnp.where(lane < ROUTE_LANE0 + N_EXPERTS, (lane - ROUTE_LANE0) // EXPERTS_PER_GROUP, -1).astype(F32)
    el1 = jnp.where(grp == gi, lg, ninf)
    v1 = jnp.max(el1, axis=-1, keepdims=True)
    i1 = jnp.min(jnp.where(el1 == v1, lanef, 1e9), axis=-1, keepdims=True)
    el2 = jnp.where(lanef == i1, ninf, el1)
    v2 = jnp.max(el2, axis=-1, keepdims=True)
    i2 = jnp.min(jnp.where(el2 == v2, lanef, 1e9), axis=-1, keepdims=True)
    t = jnp.exp(v2 - v1)
    ew1 = 1.0 / (1.0 + t)
    ew2 = t * ew1
    return gwt * (jnp.where(lanef == i1, ew1, 0.0) + jnp.where(lanef == i2, ew2, 0.0)), gi


def _moe_kernel(x_ref, g_ref, mod_ref, wr_ref, br_ref, w1_ref, w3_ref, w2_ref, o_ref,
                hs_scr, cs_scr, outs_scr, pt_scr, seg_smem, *, ne):
    e = pl.program_id(1)
    tm = x_ref.shape[0]
    tmp = hs_scr.shape[0]
    blk = MOE_BLOCK
    lane = lax.broadcasted_iota(jnp.int32, (1, LANES), 1)

    @pl.when(e == 0)
    def _():
        x = x_ref[...]
        ms = jnp.mean(x * x, axis=-1, keepdims=True)
        y = x * lax.rsqrt(ms + EPS) * g_ref[...]
        m = mod_ref[0, 0]
        h = y * (1.0 + m[4:5]) + m[3:4]
        hi, lo = _split_bf16(h)
        lg = (jnp.dot(hi, wr_ref[0, 0], preferred_element_type=F32)
              + jnp.dot(lo, wr_ref[0, 0], preferred_element_type=F32)
              + jnp.dot(hi, wr_ref[0, 1], preferred_element_type=F32)) + br_ref[0]
        comb, gi = _route(lg)
        onehot = lane.astype(F32) == gi
        ri = lax.broadcasted_iota(jnp.int32, (tm, tm), 0)
        ci = lax.broadcasted_iota(jnp.int32, (tm, tm), 1)
        earlier = jnp.where(ri > ci, 1.0, 0.0).astype(BF16)
        before = jnp.dot(earlier, jnp.where(onehot, 1.0, 0.0).astype(BF16), preferred_element_type=F32)
        rank = jnp.sum(jnp.where(onehot, before, 0.0), axis=-1, keepdims=True)
        cnt = jnp.sum(jnp.where(onehot, 1.0, 0.0), axis=0, keepdims=True)
        padded = jnp.floor((cnt + (blk - 1)) * (1.0 / blk)) * blk
        start = sum(pltpu.roll(padded, s, 1) for s in range(1, N_EXPERT_GROUPS))
        pos = rank + jnp.sum(jnp.where(onehot, start, 0.0), axis=-1, keepdims=True)
        col = lax.broadcasted_iota(jnp.int32, (tm, tmp), 1).astype(F32)
        pt_scr[...] = jnp.where(col == pos, 1.0, 0.0).astype(BF16)
        pos_row = jnp.broadcast_to(pos, (tm, LANES)).T[0:1, :]
        row = lax.broadcasted_iota(jnp.int32, (tmp, tm), 0).astype(F32)
        perm = jnp.where(row == pos_row, 1.0, 0.0).astype(BF16)
        hs_scr[...] = jnp.dot(perm, hi, preferred_element_type=F32).astype(BF16)
        chi, clo = _split_bf16(comb)
        cs_scr[...] = (jnp.dot(perm, chi, preferred_element_type=F32)
                       + jnp.dot(perm, clo, preferred_element_type=F32))
        outs_scr[...] = jnp.zeros_like(outs_scr)
        for g in range(N_EXPERT_GROUPS):
            seg_smem[g] = jnp.sum(jnp.where(lane == g, start, 0.0)).astype(jnp.int32)
            seg_smem[N_EXPERT_GROUPS + g] = jnp.sum(jnp.where(lane == g, padded, 0.0) * (1.0 / blk)).astype(jnp.int32)

    first = e * MOE_EXPERTS_PER_STEP
    grp = first // EXPERTS_PER_GROUP
    seg_start = seg_smem[grp]

    def block(j, carry):
        r = pl.multiple_of(seg_start + j * blk, blk)
        hb = hs_scr[pl.ds(r, blk), :]
        cw = cs_scr[pl.ds(r, blk), :]
        y = None
        for k in range(MOE_EXPERTS_PER_STEP):
            a = jnp.dot(hb, w1_ref[0, k], preferred_element_type=F32)
            b = jnp.dot(hb, w3_ref[0, k], preferred_element_type=F32)
            ce = jnp.sum(jnp.where(lane == first + k + ROUTE_LANE0, cw, 0.0), axis=-1, keepdims=True)
            hid = ((a * _sigmoid(a)) * b * ce).astype(BF16)
            yk = jnp.dot(hid, w2_ref[0, k], preferred_element_type=F32)
            y = yk if y is None else y + yk
        outs_scr[pl.ds(r, blk), :] += y
        return carry

    lax.fori_loop(0, seg_smem[N_EXPERT_GROUPS + grp], block, 0)

    @pl.when(e == ne // MOE_EXPERTS_PER_STEP - 1)
    def _():
        y = jnp.dot(pt_scr[...], outs_scr[...].astype(BF16), preferred_element_type=F32)
        o_ref[...] = x_ref[...] + mod_ref[0, 0][5:6] * y


def _moe(x, g, mod_all, wr, br, w1, w3, w2, layer, mod_row, tm):
    n, d = x.shape
    ne, ff = w1.shape[1], w1.shape[3]
    tmp = tm + N_EXPERT_GROUPS * MOE_BLOCK
    return pl.pallas_call(
        functools.partial(_moe_kernel, ne=ne),
        grid=(n // tm, ne // MOE_EXPERTS_PER_STEP),
        in_specs=[
            pl.BlockSpec((tm, d), lambda i, e: (i, 0)),
            pl.BlockSpec((1, d), lambda i, e: (0, 0)),
            pl.BlockSpec((1, 1, N_MOD, d), lambda i, e: (layer, mod_row(i), 0, 0)),
            pl.BlockSpec((1, 2, d, LANES), lambda i, e: (layer, 0, 0, 0)),
            pl.BlockSpec((1, 1, LANES), lambda i, e: (layer, 0, 0)),
            pl.BlockSpec((1, MOE_EXPERTS_PER_STEP, d, ff), lambda i, e: (layer, e, 0, 0)),
            pl.BlockSpec((1, MOE_EXPERTS_PER_STEP, d, ff), lambda i, e: (layer, e, 0, 0)),
            pl.BlockSpec((1, MOE_EXPERTS_PER_STEP, ff, d), lambda i, e: (layer, e, 0, 0)),
        ],
        out_specs=pl.BlockSpec((tm, d), lambda i, e: (i, 0)),
        out_shape=jax.ShapeDtypeStruct((n, d), F32),
        scratch_shapes=[pltpu.VMEM((tmp, d), BF16), pltpu.VMEM((tmp, LANES), F32), pltpu.VMEM((tmp, d), F32),
                        pltpu.VMEM((tm, tmp), BF16), pltpu.SMEM((2 * N_EXPERT_GROUPS,), jnp.int32)],
        compiler_params=_params("parallel", "arbitrary", vmem=VMEM_LIMIT_MOE),
        name="moe",
    )(x, g.reshape(1, d), mod_all, wr, br, w1, w3, w2)


def _fnorm_kernel(x_ref, g_ref, o_ref):
    x = x_ref[...]
    ms = jnp.mean(x * x, axis=-1, keepdims=True)
    o_ref[...] = x * lax.rsqrt(ms + EPS) * g_ref[...]


def _final_norm(x, g, row0, nrows, tm):
    d = x.shape[1]
    r0 = row0 // tm
    return pl.pallas_call(
        _fnorm_kernel,
        grid=(nrows // tm,),
        in_specs=[pl.BlockSpec((tm, d), lambda i: (r0 + i, 0)), pl.BlockSpec((1, d), lambda i: (0, 0))],
        out_specs=pl.BlockSpec((tm, d), lambda i: (i, 0)),
        out_shape=jax.ShapeDtypeStruct((nrows, d), F32),
        compiler_params=_params("parallel"),
        name="final_norm",
    )(x, g.reshape(1, d))


def _rope_tables(dec_seq):
    t = jnp.arange(dec_seq, dtype=jnp.int32)
    half = ROPE_AXIS_DIM // 2
    freqs = ROPE_THETA ** (-jnp.arange(half, dtype=F32) / half)

    def axis(pos):
        ang = pos.astype(F32)[:, None] * freqs[None, :]
        c, s = jnp.cos(ang), jnp.sin(ang)
        return jnp.concatenate([c, c], axis=-1), jnp.concatenate([-s, s], axis=-1)

    cr, sr = axis(t // GRID_W)
    cc, sc = axis(t % GRID_W)
    cos = jnp.tile(jnp.concatenate([cr, cc], axis=-1), (1, ATT_HEADS))
    sin = jnp.tile(jnp.concatenate([sr, sc], axis=-1), (1, ATT_HEADS))
    return cos, sin


def _block_diag_ones(width, block):
    i = np.arange(width) // block
    return jnp.asarray((i[:, None] == i[None, :]).astype(np.float32), dtype=BF16)


def _pack_w_in(w_in):
    depth, d, _ = w_in.shape
    seg = lambda a, b: w_in[:, :, a:b]
    zeros = lambda n: jnp.zeros((depth, d, n), w_in.dtype)
    parts = [seg(SRC_XBC, SRC_DT), seg(SRC_GLU, SRC_FOUR), seg(SRC_Z, SRC_XBC), seg(SRC_Q, SRC_K),
             seg(SRC_FOUR, SRC_END), seg(SRC_K, SRC_V), seg(SRC_V, SRC_GLU),
             seg(SRC_DT, SRC_DT + SSD_HEADS), zeros(LANES - SSD_HEADS),
             seg(SRC_DT + SSD_HEADS, SRC_Q), zeros(LANES - SSD_HEADS)]
    return jnp.concatenate(parts, axis=-1).astype(BF16)


def _lanes8(v):
    return jnp.pad(v.astype(F32), ((0, 0), (0, 0), (0, LANES - SSD_HEADS)))[:, :, None, :]


def kernel(x_prompt, x_sample, cache_k, cache_v, state_ssd, c, c_ctx, norm1_g, norm2_g, w_mod, b_mod, w_in,
           ssd_conv_w, ssd_conv_b, ssd_dt_bias, ssd_A_log, ssd_D, ssd_norm_g, q_norm_g, k_norm_g, cf_dw_w, cf_dw_b,
           cf_ln_g, cf_ln_b, w_out, router_group_w, router_group_b, router_expert_w, router_expert_b, w1, w3, w2,
           final_norm_g):
    bc, lc, d = x_prompt.shape
    bl, ll, _ = x_sample.shape
    n_ctx, n_lat = bc * lc, bl * ll
    n = n_ctx + n_lat
    depth = w_in.shape[0]
    kvw = ATT_KV_HEADS * HEAD_DIM
    assert lc & (lc - 1) == 0 and ll & (ll - 1) == 0 and lc % SSD_CHUNK == 0 and ll % SSD_CHUNK == 0
    assert n_ctx % ll == 0 and ll % GRID_W == 0

    tm = _pow2_tile(1024, n_ctx, ll)
    tm_moe = _pow2_tile(512, n_ctx, ll)
    tm_qk = _pow2_tile(512, lc, ll)

    def mod_row_fn(t):
        nb, per = n_ctx // t, ll // t
        return lambda i: jnp.where(i < nb, 0, 1 + (i - nb) // per)

    w_in_p = _pack_w_in(w_in)
    w_out_p = w_out.astype(BF16).reshape(depth, 4, GROUP_W, d)
    w1_b, w3_b, w2_b = w1.astype(BF16), w3.astype(BF16), w2.astype(BF16)
    wr = jnp.concatenate([router_group_w, router_expert_w,
                          jnp.zeros((depth, d, LANES - N_EXPERT_GROUPS - N_EXPERTS), F32)], axis=-1)
    wr_hi = wr.astype(BF16)
    wr_lo = (wr - wr_hi.astype(F32)).astype(BF16)
    wr_p = jnp.stack([wr_hi, wr_lo], axis=1)
    br_p = jnp.concatenate([router_group_b, router_expert_b,
                            jnp.zeros((depth, LANES - N_EXPERT_GROUPS - N_EXPERTS), F32)], axis=-1)[:, None, :]
    dt_bias_p = _lanes8(ssd_dt_bias)
    a_log_p = _lanes8(ssd_A_log)
    d_lanes = jnp.repeat(ssd_D.astype(F32), SSD_HEADDIM, axis=-1)[:, None, :]
    expand_np = np.arange(LANES)[:, None] == (np.arange(512)[None, :] // SSD_HEADDIM)
    expand = jnp.asarray(expand_np, dtype=BF16)
    expand_t = jnp.asarray(expand_np.T, dtype=BF16)
    gq = jnp.tile(q_norm_g, (1, ATT_HEADS))[:, None, :]
    gk = jnp.tile(k_norm_g, (1, ATT_KV_HEADS))[:, None, :]
    bdq = _block_diag_ones(512, HEAD_DIM)
    bdk = _block_diag_ones(kvw, HEAD_DIM)
    rope_tabs = _rope_tables(ll)
    cch, sch = _dft_tables(FNET_GROUP_CH)
    eye4 = jnp.eye(4, dtype=F32)
    wch = jnp.concatenate([jnp.kron(eye4, cch), jnp.kron(eye4, sch)], axis=-1).astype(BF16)
    dft_c = {}
    for lseq in (lc, ll):
        cl, sl = _dft_tables(lseq)
        dft_c[lseq] = (cl.astype(BF16), (-sl).astype(BF16))
    kz_cache = _cache_variants(cache_k, ones_tail=False)
    vz_cache = _cache_variants(cache_v, ones_tail=True)

    mod_all = _adaln(jnp.concatenate([c_ctx[None, :], c], axis=0), w_mod, b_mod)

    x = jnp.concatenate([x_prompt.reshape(n_ctx, d), x_sample.reshape(n_lat, d)], axis=0)
    h0_ctx = jnp.zeros((bc, 2, 512, SSD_D_STATE), F32)
    h0_lat = state_ssd.reshape(bl, depth, 2, 512, SSD_D_STATE)
    ks, vs, ss = [], [], []
    for l in range(depth):
        u = _inproj(x, norm1_g[l], mod_all, w_in_p, l, mod_row_fn(tm), tm)

        xbc = _ssdpre(u, ssd_conv_w[l], ssd_conv_b[l], n_ctx, lc, ll)
        yf_c, yb_c, h_c = _ssd_scan(xbc, u, dt_bias_p[l], a_log_p[l], expand, expand_t, h0_ctx, 0, bc, lc)
        yf_l, yb_l, _ = _ssd_scan(xbc, u, dt_bias_p[l], a_log_p[l], expand, expand_t, h0_lat[:, l], n_ctx, bl, ll)
        y_ssd = _gnorm((yf_c, yb_c), (yf_l, yb_l), xbc, u, d_lanes[l], ssd_norm_g[l], tm)
        ss.append(h_c.reshape(bc, 2, SSD_HEADS, SSD_HEADDIM, SSD_D_STATE))

        q_c, kz_c, vz_c, k_c, v_c = _qk_prep(u, gq[l], gk[l], bdq, bdk, 0, bc, lc, tm_qk, emit_kv=True)
        q_l, kz_l, vz_l = _qk_prep(u, gq[l], gk[l], bdq, bdk, n_ctx, bl, ll, tm_qk, tables=rope_tabs)
        ks.append(k_c.reshape(bc, lc, ATT_KV_HEADS, HEAD_DIM))
        vs.append(v_c.reshape(bc, lc, ATT_KV_HEADS, HEAD_DIM))
        att_c = _attention(q_c, kz_c, vz_c, min(256, lc))
        att_l = _attention(q_l, kz_l, vz_l, min(512, ll), cache=(kz_cache, vz_cache, l))

        y_conv = _conformer(u, cf_dw_w[l], cf_dw_b[l], cf_ln_g[l], cf_ln_b[l], n_ctx, lc, ll)

        zf = _fnet_channels(u, wch, tm)
        four_c = _fnet_positions(zf, *dft_c[lc], 0, bc, lc)
        four_l = _fnet_positions(zf, *dft_c[ll], n_ctx, bl, ll)

        x = _outproj(y_ssd, y_conv, att_c, att_l, four_c, four_l, w_out_p, x, mod_all, l, mod_row_fn(tm), tm)
        x = _moe(x, norm2_g[l], mod_all, wr_p, br_p, w1_b, w3_b, w2_b, l, mod_row_fn(tm_moe), tm_moe)

    y_prompt = _final_norm(x, final_norm_g, 0, n_ctx, tm).reshape(bc, lc, d)
    y_sample = _final_norm(x, final_norm_g, n_ctx, n_lat, tm).reshape(bl, ll, d)
    return (y_prompt, y_sample, jnp.stack(ks, axis=1), jnp.stack(vs, axis=1), jnp.stack(ss, axis=1))
```
